```python
import math
import jax, jax.numpy as jnp
from jax import lax
import numpy as np

D_MODEL = 1024
BATCH = 8
SEQ = 4096
DEPTH = 4

HEAD_DIM = 64
ATTN_Q_HEADS = D_MODEL // (2 * HEAD_DIM)
ATTN_KV_HEADS = max(1, ATTN_Q_HEADS // 4)
RET_HEADS = D_MODEL // (2 * HEAD_DIM)
ATTN_WIDTH = ATTN_Q_HEADS * HEAD_DIM
KV_WIDTH = ATTN_KV_HEADS * HEAD_DIM
RET_WIDTH = RET_HEADS * HEAD_DIM
MIX_WIDTH = ATTN_WIDTH + RET_WIDTH
IN_SPLITS = (ATTN_WIDTH,
             ATTN_WIDTH + KV_WIDTH,
             ATTN_WIDTH + 2 * KV_WIDTH,
             ATTN_WIDTH + 2 * KV_WIDTH + RET_WIDTH,
             ATTN_WIDTH + 2 * KV_WIDTH + 2 * RET_WIDTH,
             ATTN_WIDTH + 2 * KV_WIDTH + 3 * RET_WIDTH)
IN_WIDTH = ATTN_WIDTH + 2 * KV_WIDTH + 4 * RET_WIDTH
WINDOW = 128
ATTN_BLOCK = 128
RET_CHUNK = 128
ROPE_THETA = 10000.0
RET_THETA = 10000.0
D_FF = ((8 * D_MODEL // 3 + 127) // 128) * 128
LN_EPS = 1e-5
GN_EPS = 1e-6
DEEPNORM_ALPHA = (2 * DEPTH) ** 0.25
DEEPNORM_BETA = (8 * DEPTH) ** -0.25

kernel_name = "hybrid_swa_sink_retention_macaron_deepnorm"


def layer_norm(x, g, b):
    xf = x.astype(jnp.float32)
    mu = jnp.mean(xf, axis=-1, keepdims=True)
    var = jnp.mean(jnp.square(xf - mu), axis=-1, keepdims=True)
    return ((xf - mu) * lax.rsqrt(var + LN_EPS)).astype(x.dtype) * g + b


def swiglu_ffn(x, w_gu, w_down):
    a, u = jnp.split(x @ w_gu, 2, axis=-1)
    return (jax.nn.silu(a) * u) @ w_down


def rope_tables(seq):
    pos = jnp.arange(seq, dtype=jnp.float32)
    inv_freq = ROPE_THETA ** (-jnp.arange(0, HEAD_DIM, 2, dtype=jnp.float32) / HEAD_DIM)
    ang = pos[:, None] * inv_freq[None, :]
    return jnp.cos(ang), jnp.sin(ang)


def retention_rotation_tables(seq):
    pos = jnp.arange(seq, dtype=jnp.float32)
    ang_freq = 1.0 / (RET_THETA ** jnp.linspace(0.0, 1.0, HEAD_DIM // 2, dtype=jnp.float32))
    ang = pos[:, None] * ang_freq[None, :]
    return jnp.cos(ang), jnp.sin(ang)


def apply_rope(x, cos, sin):
    c = cos[:, None, :].astype(x.dtype)
    s = sin[:, None, :].astype(x.dtype)
    x1, x2 = jnp.split(x, 2, axis=-1)
    return jnp.concatenate([x1 * c - x2 * s, x2 * c + x1 * s], axis=-1)


def apply_pair_rotation(x, cos, sin):
    c = cos[:, None, :].astype(x.dtype)
    s = sin[:, None, :].astype(x.dtype)
    xe = x[..., 0::2]
    xo = x[..., 1::2]
    return jnp.stack([xe * c - xo * s, xo * c + xe * s], axis=-1).reshape(x.shape)


def sliding_window_sink_attention(q, k, v, sinks):
    B, S, Hq, D = q.shape
    Hkv = k.shape[2]
    G = Hq // Hkv
    C = ATTN_BLOCK
    N = S // C
    qb = (q * (HEAD_DIM ** -0.5)).reshape(B, N, C, Hkv, G, D)

    def with_prev_block(t):
        tb = t.reshape(B, N, C, Hkv, D)
        prev = jnp.pad(tb, ((0, 0), (1, 0), (0, 0), (0, 0), (0, 0)))[:, :-1]
        return jnp.concatenate([prev, tb], axis=2)

    kk = with_prev_block(k)
    vv = with_prev_block(v)
    s = jnp.einsum('bnqhgd,bnkhd->bnhgqk', qb, kk).astype(jnp.float32)
    q_idx = jnp.arange(C)[:, None]
    k_rel = jnp.arange(2 * C)[None, :] - C
    rel = q_idx - k_rel
    band = (rel >= 0) & (rel < WINDOW)
    valid = (jnp.arange(N)[:, None] * C + k_rel) >= 0
    mask = band[None, :, :] & valid[:, None, :]
    s = jnp.where(mask[None, :, None, None, :, :], s, -jnp.inf)
    sink = sinks.astype(jnp.float32).reshape(Hkv, G)[None, None, :, :, None, None]
    m = jnp.maximum(jnp.max(s, axis=-1, keepdims=True), sink)
    p = jnp.exp(s - m)
    denom = jnp.sum(p, axis=-1, keepdims=True) + jnp.exp(sink - m)
    o = jnp.einsum('bnhgqk,bnkhd->bnqhgd', (p / denom).astype(v.dtype), vv)
    return o.reshape(B, S, Hq * D)


def multiscale_retention(q, k, v, gate):
    B, S, H, D = q.shape
    C = RET_CHUNK
    N = S // C
    log_gamma = jnp.log1p(-jnp.exp2(-5.0 - jnp.arange(H, dtype=jnp.float32)))
    idx = jnp.arange(C, dtype=jnp.float32)
    diff = idx[:, None] - idx[None, :]
    decay = jnp.where(diff[None] >= 0,
                      jnp.exp(jnp.maximum(diff, 0.0)[None] * log_gamma[:, None, None]), 0.0)
    w_k = jnp.exp((C - 1.0 - idx)[None, :] * log_gamma[:, None])
    w_q = jnp.exp((idx + 1.0)[None, :] * log_gamma[:, None])
    g_chunk = jnp.exp(C * log_gamma)
    qc = q.reshape(B, N, C, H, D)
    kc = k.reshape(B, N, C, H, D)
    vc = v.reshape(B, N, C, H, D)
    scores = jnp.einsum('bnihd,bnjhd->bnhij', qc, kc) * decay
    intra = jnp.einsum('bnhij,bnjhe->bnihe', scores, vc)
    kv = jnp.einsum('bnjhd,hj,bnjhe->bnhde', kc, w_k, vc)

    def step(state, kv_n):
        return state * g_chunk[None, :, None, None] + kv_n, state

    init = jnp.zeros((B, H, D, D), dtype=kv.dtype)
    _, states = lax.scan(step, init, jnp.moveaxis(kv, 1, 0))
    states = jnp.moveaxis(states, 0, 1)
    cross = jnp.einsum('bnihd,hi,bnhde->bnihe', qc, w_q, states)
    o = (intra + cross).reshape(B, S, H, D).astype(jnp.float32)
    mu = jnp.mean(o, axis=-1, keepdims=True)
    var = jnp.mean(jnp.square(o - mu), axis=-1, keepdims=True)
    o = ((o - mu) * lax.rsqrt(var + GN_EPS)).reshape(B, S, H * D).astype(gate.dtype)
    return jax.nn.silu(gate) * o


def hybrid_mixer(x, w_in, w_out, sinks, rope_cos, rope_sin, ret_cos, ret_sin):
    B, S, _ = x.shape
    h = x @ w_in
    qa, ka, va, qr, kr, vr, gr = jnp.split(h, IN_SPLITS, axis=-1)
    qa = apply_rope(qa.reshape(B, S, ATTN_Q_HEADS, HEAD_DIM), rope_cos, rope_sin)
    ka = apply_rope(ka.reshape(B, S, ATTN_KV_HEADS, HEAD_DIM), rope_cos, rope_sin)
    va = va.reshape(B, S, ATTN_KV_HEADS, HEAD_DIM)
    y_attn = sliding_window_sink_attention(qa, ka, va, sinks)
    qr = apply_pair_rotation(qr.reshape(B, S, RET_HEADS, HEAD_DIM), ret_cos, ret_sin)
    kr = apply_pair_rotation(kr.reshape(B, S, RET_HEADS, HEAD_DIM), ret_cos, ret_sin) * (HEAD_DIM ** -0.5)
    y_ret = multiscale_retention(qr, kr, vr.reshape(B, S, RET_HEADS, HEAD_DIM), gr)
    return jnp.concatenate([y_attn, y_ret], axis=-1) @ w_out


def setup_inputs(seed: int = 0) -> dict:
    key = jax.random.key(seed)
    ks = jax.random.split(key, 16)
    f32 = jnp.float32
    x = jax.random.normal(ks[0], (BATCH, SEQ, D_MODEL), f32)
    col_scale = jnp.concatenate([
        jnp.ones((ATTN_WIDTH + KV_WIDTH,), f32),
        jnp.full((KV_WIDTH,), DEEPNORM_BETA, f32),
        jnp.ones((2 * RET_WIDTH,), f32),
        jnp.full((RET_WIDTH,), DEEPNORM_BETA, f32),
        jnp.ones((RET_WIDTH,), f32)])
    w_in = jax.random.normal(ks[1], (DEPTH, D_MODEL, IN_WIDTH), f32) * (D_MODEL ** -0.5) * col_scale
    w_out = jax.random.normal(ks[2], (DEPTH, MIX_WIDTH, D_MODEL), f32) * (MIX_WIDTH ** -0.5) * DEEPNORM_BETA
    attn_sinks = 0.5 * jax.random.normal(ks[3], (DEPTH, ATTN_Q_HEADS), f32)
    ffn1_w_gu = jax.random.normal(ks[4], (DEPTH, D_MODEL, 2 * D_FF), f32) * (D_MODEL ** -0.5) * DEEPNORM_BETA
    ffn1_w_down = jax.random.normal(ks[5], (DEPTH, D_FF, D_MODEL), f32) * (D_FF ** -0.5) * DEEPNORM_BETA
    ffn2_w_gu = jax.random.normal(ks[6], (DEPTH, D_MODEL, 2 * D_FF), f32) * (D_MODEL ** -0.5) * DEEPNORM_BETA
    ffn2_w_down = jax.random.normal(ks[7], (DEPTH, D_FF, D_MODEL), f32) * (D_FF ** -0.5) * DEEPNORM_BETA
    ln1_g = 1.0 + 0.02 * jax.random.normal(ks[8], (DEPTH, D_MODEL), f32)
    ln1_b = 0.02 * jax.random.normal(ks[9], (DEPTH, D_MODEL), f32)
    ln2_g = 1.0 + 0.02 * jax.random.normal(ks[10], (DEPTH, D_MODEL), f32)
    ln2_b = 0.02 * jax.random.normal(ks[11], (DEPTH, D_MODEL), f32)
    ln3_g = 1.0 + 0.02 * jax.random.normal(ks[12], (DEPTH, D_MODEL), f32)
    ln3_b = 0.02 * jax.random.normal(ks[13], (DEPTH, D_MODEL), f32)
    return {"x": x, "w_in": w_in, "w_out": w_out, "attn_sinks": attn_sinks,
            "ffn1_w_gu": ffn1_w_gu, "ffn1_w_down": ffn1_w_down,
            "ffn2_w_gu": ffn2_w_gu, "ffn2_w_down": ffn2_w_down,
            "ln1_g": ln1_g, "ln1_b": ln1_b, "ln2_g": ln2_g, "ln2_b": ln2_b,
            "ln3_g": ln3_g, "ln3_b": ln3_b}


def reference(x, w_in, w_out, attn_sinks, ffn1_w_gu, ffn1_w_down, ffn2_w_gu, ffn2_w_down,
              ln1_g, ln1_b, ln2_g, ln2_b, ln3_g, ln3_b):
    S = x.shape[1]
    rope_cos, rope_sin = rope_tables(S)
    ret_cos, ret_sin = retention_rotation_tables(S)
    for l in range(DEPTH):
        x = layer_norm(DEEPNORM_ALPHA * x + 0.5 * swiglu_ffn(x, ffn1_w_gu[l], ffn1_w_down[l]),
                       ln1_g[l], ln1_b[l])
        x = layer_norm(DEEPNORM_ALPHA * x + hybrid_mixer(x, w_in[l], w_out[l], attn_sinks[l],
                                                         rope_cos, rope_sin, ret_cos, ret_sin),
                       ln2_g[l], ln2_b[l])
        x = layer_norm(DEEPNORM_ALPHA * x + 0.5 * swiglu_ffn(x, ffn2_w_gu[l], ffn2_w_down[l]),
                       ln3_g[l], ln3_b[l])
    return x
```

```python
import functools

import numpy as np
import jax
import jax.numpy as jnp
from jax import lax
from jax.experimental import pallas as pl
from jax.experimental.pallas import tpu as pltpu

F32 = jnp.float32
BF16 = jnp.bfloat16

D_MODEL = 1024
DEPTH = 4
HEAD_DIM = 64
HALF = HEAD_DIM // 2
ATTN_Q_HEADS = 8
ATTN_KV_HEADS = 2
RET_HEADS = 8
ATTN_WIDTH = ATTN_Q_HEADS * HEAD_DIM
KV_WIDTH = ATTN_KV_HEADS * HEAD_DIM
RET_WIDTH = RET_HEADS * HEAD_DIM
WINDOW = 128
CHUNK = 128
ROPE_THETA = 10000.0
RET_THETA = 10000.0
D_FF = 2816
LN_EPS = 1e-5
GN_EPS = 1e-6
ALPHA = (2 * DEPTH) ** 0.25
QK_SCALE = HEAD_DIM ** -0.5

LANES = 128
PAIR = 2 * HEAD_DIM
N_PAIRS = RET_HEADS // 2

QA0, KA0, VA0 = 0, 512, 768
QR0, KR0, VR0, GR0 = 1024, 1536, 2048, 2560
IN_COLS = 3072

VMEM_LIMIT_BYTES = 56 * 1024 * 1024


def _in_proj_column_permutation():
    first = np.arange(HALF)
    cols = []
    for p in range(ATTN_Q_HEADS // 2):
        a, b = 2 * p * HEAD_DIM, (2 * p + 1) * HEAD_DIM
        cols += [a + first, b + first, a + HALF + first, b + HALF + first]
    k0 = ATTN_WIDTH
    for g in range(ATTN_KV_HEADS):
        a = k0 + g * HEAD_DIM
        cols += [a + first, a + first, a + HALF + first, a + HALF + first]
    v0 = ATTN_WIDTH + KV_WIDTH
    for g in range(ATTN_KV_HEADS):
        a = v0 + g * HEAD_DIM + np.arange(HEAD_DIM)
        cols += [a, a]
    for base in (ATTN_WIDTH + 2 * KV_WIDTH, ATTN_WIDTH + 2 * KV_WIDTH + RET_WIDTH):
        for p in range(N_PAIRS):
            a, b = base + 2 * p * HEAD_DIM, base + (2 * p + 1) * HEAD_DIM
            cols += [a + 2 * first, b + 2 * first, a + 2 * first + 1, b + 2 * first + 1]
    vr0 = ATTN_WIDTH + 2 * KV_WIDTH + 2 * RET_WIDTH
    cols += [vr0 + np.arange(2 * RET_WIDTH)]
    perm = np.concatenate(cols).astype(np.int32)
    assert perm.shape == (IN_COLS,)
    return perm


_IN_PERM = _in_proj_column_permutation()


def _rotation_tables(seq):
    pos = jnp.arange(seq, dtype=F32)
    inv_freq = ROPE_THETA ** (-jnp.arange(0, HEAD_DIM, 2, dtype=F32) / HEAD_DIM)
    ang = pos[:, None] * inv_freq[None, :]
    ret_freq = 1.0 / (RET_THETA ** jnp.linspace(0.0, 1.0, HEAD_DIM // 2, dtype=F32))
    rang = pos[:, None] * ret_freq[None, :]

    def lay(c, s):
        return (jnp.concatenate([c, c, c, c], axis=1),
                jnp.concatenate([-s, -s, s, s], axis=1))

    return lay(jnp.cos(ang), jnp.sin(ang)) + lay(jnp.cos(rang), jnp.sin(rang))


def _retention_tables():
    h = jnp.arange(RET_HEADS, dtype=F32)
    log_gamma = jnp.log1p(-jnp.exp2(-5.0 - h))
    idx = jnp.arange(CHUNK, dtype=F32)
    diff = idx[:, None] - idx[None, :]
    decay = jnp.where(diff[None] >= 0,
                      jnp.exp(jnp.maximum(diff, 0.0)[None] * log_gamma[:, None, None]), 0.0)
    decay = jnp.transpose(decay, (1, 0, 2)).reshape(CHUNK, RET_HEADS * CHUNK)
    w_k = jnp.exp((CHUNK - 1.0 - idx)[None, :] * log_gamma[:, None])
    w_q = jnp.exp((idx + 1.0)[None, :] * log_gamma[:, None])
    g_chunk = jnp.exp(CHUNK * log_gamma)

    def rolled(w):
        w = w.T.reshape(CHUNK, N_PAIRS, 2, 1)
        return jnp.broadcast_to(jnp.concatenate([w, w], axis=2), (CHUNK, N_PAIRS, 4, HALF)).reshape(CHUNK, RET_WIDTH)

    g_nat = jnp.repeat(g_chunk, HEAD_DIM)[None, :]
    return decay, rolled(w_q), rolled(w_k), g_nat


def _layer_norm(z, g, b):
    mu = jnp.mean(z, axis=-1, keepdims=True)
    d = z - mu
    var = jnp.mean(d * d, axis=-1, keepdims=True)
    return d * lax.rsqrt(var + LN_EPS) * g + b


def _dot(a, b):
    return jnp.dot(a, b, preferred_element_type=F32)


def _dot_nt(a, b):
    return lax.dot_general(a, b, (((1,), (1,)), ((), ())), preferred_element_type=F32)


def _dot_tn(a, b):
    return lax.dot_general(a, b, (((0,), (0,)), ((), ())), preferred_element_type=F32)


def _ffn_kernel(x_ref, wgu_ref, wd_ref, g_ref, b_ref, o_ref, h_scr, *, tf):
    x = x_ref[...]
    xb = x.astype(BF16)
    for c in range(D_FF // tf):
        a = _dot(xb, wgu_ref[:, c * tf:(c + 1) * tf])
        u = _dot(xb, wgu_ref[:, D_FF + c * tf:D_FF + (c + 1) * tf])
        h_scr[:, c * tf:(c + 1) * tf] = (jax.nn.silu(a) * u).astype(BF16)
    y = _dot(h_scr[...], wd_ref[...])
    o_ref[...] = _layer_norm(ALPHA * x + 0.5 * y, g_ref[...], b_ref[...])


def _resident(shape):
    return pl.BlockSpec(shape, lambda *_: (0,) * len(shape), pipeline_mode=pl.Buffered(1))


def _ffn(x, wgu, wd, g, b, *, tm, tf):
    t = x.shape[0]
    return pl.pallas_call(
        functools.partial(_ffn_kernel, tf=tf),
        grid=(t // tm,),
        in_specs=[pl.BlockSpec((tm, D_MODEL), lambda i: (i, 0)),
                  _resident((D_MODEL, 2 * D_FF)),
                  _resident((D_FF, D_MODEL)),
                  _resident((1, D_MODEL)),
                  _resident((1, D_MODEL))],
        out_specs=pl.BlockSpec((tm, D_MODEL), lambda i: (i, 0)),
        out_shape=jax.ShapeDtypeStruct((t, D_MODEL), F32),
        scratch_shapes=[pltpu.VMEM((tm, D_FF), BF16)],
        compiler_params=pltpu.CompilerParams(dimension_semantics=("arbitrary",),
                                             vmem_limit_bytes=VMEM_LIMIT_BYTES),
        name="ffn",
    )(x, wgu, wd, g, b)


def _rotate(h, cos, sin):
    outs = []
    for j in range(h.shape[1] // LANES):
        hj = h[:, j * LANES:(j + 1) * LANES]
        outs.append(hj * cos + pltpu.roll(hj, PAIR // 2, 1) * sin)
    return jnp.concatenate(outs, axis=1)


def _mixer_kernel(sinks_ref, x_ref, win_ref, wout_ref, g_ref, b_ref,
                  rcos_ref, rsin_ref, tcos_ref, tsin_ref,
                  decay_ref, wq_ref, wk_ref, gch_ref,
                  o_ref,
                  qa_s, ka_s, va_s, qr_s, kr_s, vr_s, gr_s, y_s, state_s, *, ts):
    s_idx = pl.program_id(1)
    n_chunks = ts // CHUNK

    @pl.when(s_idx == 0)
    def _():
        ka_s[0:CHUNK, :] = jnp.zeros((CHUNK, 2 * PAIR), BF16)
        va_s[0:CHUNK, :] = jnp.zeros((CHUNK, 2 * PAIR), BF16)
        state_s[...] = jnp.zeros(state_s.shape, F32)

    x = x_ref[...]
    xb = x.astype(BF16)
    rcos, rsin = rcos_ref[...], rsin_ref[...]
    tcos, tsin = tcos_ref[...], tsin_ref[...]
    h = _dot(xb, win_ref[:, QA0:QA0 + ATTN_WIDTH])
    qa_s[...] = (_rotate(h, rcos, rsin) * QK_SCALE).astype(BF16)
    h = _dot(xb, win_ref[:, KA0:KA0 + 4 * PAIR])
    ka_s[CHUNK:, :] = _rotate(h[:, :2 * PAIR], rcos, rsin).astype(BF16)
    va_s[CHUNK:, :] = h[:, 2 * PAIR:].astype(BF16)
    h = _dot(xb, win_ref[:, QR0:QR0 + RET_WIDTH])
    qr_s[...] = _rotate(h, tcos, tsin)
    h = _dot(xb, win_ref[:, KR0:KR0 + RET_WIDTH])
    kr_s[...] = _rotate(h, tcos, tsin) * QK_SCALE
    h = _dot(xb, win_ref[:, VR0:VR0 + RET_WIDTH])
    vr_s[...] = h.astype(BF16)
    gr_s[...] = _dot(xb, win_ref[:, GR0:GR0 + RET_WIDTH])

    lane = lax.broadcasted_iota(jnp.int32, (1, LANES), 1)
    slot_a = (lane % HEAD_DIM) < HALF
    lo = lane < HEAD_DIM
    row = lax.broadcasted_iota(jnp.int32, (LANES, LANES), 0)
    col = lax.broadcasted_iota(jnp.int32, (LANES, LANES), 1)
    state_mask = ((row % HEAD_DIM) < HALF) == (col < HEAD_DIM)
    gn_avg = jnp.where((row < HEAD_DIM) == (col < HEAD_DIM), 1.0 / HEAD_DIM, 0.0).astype(BF16)
    qi = lax.broadcasted_iota(jnp.int32, (CHUNK, 2 * CHUNK), 0)
    kk = lax.broadcasted_iota(jnp.int32, (CHUNK, 2 * CHUNK), 1)
    band = (kk > qi) & (kk <= qi + WINDOW)

    def split_mean(v):
        hi = v.astype(BF16)
        lo_part = (v - hi.astype(F32)).astype(BF16)
        return _dot(hi, gn_avg) + _dot(lo_part, gn_avg)

    def chunk_body(c, carry):
        r0 = pl.multiple_of(c * CHUNK, CHUNK)
        rows = pl.ds(r0, CHUNK)
        kv_rows = pl.ds(r0, 2 * CHUNK)
        has_prev = (s_idx * n_chunks + c) > 0
        mask = band & ((kk >= CHUNK) | has_prev)

        for g in range(ATTN_KV_HEADS):
            kd = ka_s[kv_rows, g * PAIR:(g + 1) * PAIR]
            vd = va_s[kv_rows, g * PAIR:(g + 1) * PAIR]
            zero = jnp.zeros_like(kd)
            k_ab = (jnp.where(slot_a, kd, zero), jnp.where(slot_a, zero, kd))
            v_ab = (jnp.where(lo, vd, zero), jnp.where(lo, zero, vd))
            for pp in range(2):
                p = 2 * g + pp
                q = qa_s[rows, p * PAIR:(p + 1) * PAIR]
                o = None
                for hh in range(2):
                    sink = sinks_ref[2 * p + hh]
                    s = jnp.where(mask, _dot_nt(q, k_ab[hh]), -jnp.inf)
                    m = jnp.maximum(jnp.max(s, axis=-1, keepdims=True), sink)
                    e = jnp.exp(s - m)
                    denom = jnp.sum(e, axis=-1, keepdims=True) + jnp.exp(sink - m)
                    part = _dot((e / denom).astype(BF16), v_ab[hh])
                    o = part if o is None else o + part
                y_s[rows, p * PAIR:(p + 1) * PAIR] = o.astype(BF16)

        for p in range(N_PAIRS):
            cols = slice(p * PAIR, (p + 1) * PAIR)
            q = qr_s[rows, cols]
            k = kr_s[rows, cols]
            v = vr_s[rows, cols]
            qb = q.astype(BF16)
            kb = k.astype(BF16)
            zero = jnp.zeros_like(kb)
            s_a = _dot_nt(qb, jnp.where(slot_a, kb, zero)) * decay_ref[:, (2 * p) * CHUNK:(2 * p + 1) * CHUNK]
            s_b = _dot_nt(qb, jnp.where(slot_a, zero, kb)) * decay_ref[:, (2 * p + 1) * CHUNK:(2 * p + 2) * CHUNK]
            st = state_s[p]
            o = (_dot(s_a.astype(BF16), jnp.where(lo, v, zero))
                 + _dot(s_b.astype(BF16), jnp.where(lo, zero, v))
                 + _dot((q * wq_ref[:, cols]).astype(BF16), st.astype(BF16)))
            kv = _dot_tn((k * wk_ref[:, cols]).astype(BF16), v)
            state_s[p] = st * gch_ref[:, cols] + jnp.where(state_mask, kv, 0.0)
            d = o - split_mean(o)
            var = split_mean(d * d)
            on = d * lax.rsqrt(var + GN_EPS)
            y_s[rows, ATTN_WIDTH + p * PAIR:ATTN_WIDTH + (p + 1) * PAIR] = (
                jax.nn.silu(gr_s[rows, cols]) * on).astype(BF16)
        return carry

    lax.fori_loop(0, n_chunks, chunk_body, 0)

    ka_s[0:CHUNK, :] = ka_s[ts:ts + CHUNK, :]
    va_s[0:CHUNK, :] = va_s[ts:ts + CHUNK, :]

    y = _dot(y_s[...], wout_ref[...])
    o_ref[...] = _layer_norm(ALPHA * x + y, g_ref[...], b_ref[...])


def _mixer(x, sinks, win, wout, g, b, rot_tables, ret_tables, *, batch, seq, ts):
    n_s = seq // ts
    rcos, rsin, tcos, tsin = rot_tables
    decay, wq, wk, gch = ret_tables
    row_spec = pl.BlockSpec((ts, D_MODEL), lambda bi, si: (bi * n_s + si, 0))
    tab_spec = pl.BlockSpec((ts, LANES), lambda bi, si: (si, 0))
    return pl.pallas_call(
        functools.partial(_mixer_kernel, ts=ts),
        grid=(batch, n_s),
        in_specs=[pl.BlockSpec(memory_space=pltpu.SMEM),
                  row_spec,
                  _resident((D_MODEL, IN_COLS)),
                  _resident((ATTN_WIDTH + RET_WIDTH, D_MODEL)),
                  _resident((1, D_MODEL)),
                  _resident((1, D_MODEL)),
                  tab_spec, tab_spec, tab_spec, tab_spec,
                  _resident((CHUNK, RET_HEADS * CHUNK)),
                  _resident((CHUNK, RET_WIDTH)),
                  _resident((CHUNK, RET_WIDTH)),
                  _resident((1, RET_WIDTH))],
        out_specs=row_spec,
        out_shape=jax.ShapeDtypeStruct((batch * seq, D_MODEL), F32),
        scratch_shapes=[pltpu.VMEM((ts, ATTN_WIDTH), BF16),
                        pltpu.VMEM((CHUNK + ts, 2 * PAIR), BF16),
                        pltpu.VMEM((CHUNK + ts, 2 * PAIR), BF16),
                        pltpu.VMEM((ts, RET_WIDTH), F32),
                        pltpu.VMEM((ts, RET_WIDTH), F32),
                        pltpu.VMEM((ts, RET_WIDTH), BF16),
                        pltpu.VMEM((ts, RET_WIDTH), F32),
                        pltpu.VMEM((ts, ATTN_WIDTH + RET_WIDTH), BF16),
                        pltpu.VMEM((N_PAIRS, LANES, LANES), F32)],
        compiler_params=pltpu.CompilerParams(dimension_semantics=("arbitrary", "arbitrary"),
                                             vmem_limit_bytes=VMEM_LIMIT_BYTES),
        name="mixer",
    )(sinks, x, win, wout, g, b, rcos, rsin, tcos, tsin, decay, wq, wk, gch)


def _pick_tile(n, want):
    t = min(n, want)
    assert n % t == 0 and t % CHUNK == 0, (n, t)
    return t


def kernel(x, w_in, w_out, attn_sinks, ffn1_w_gu, ffn1_w_down, ffn2_w_gu, ffn2_w_down,
           ln1_g, ln1_b, ln2_g, ln2_b, ln3_g, ln3_b):
    batch, seq, d = x.shape
    assert d == D_MODEL and seq % CHUNK == 0
    assert w_in.shape[0] == DEPTH
    t = batch * seq
    tm = _pick_tile(t, 512)
    ts = _pick_tile(seq, 512)

    rot_tables = _rotation_tables(seq)
    ret_tables = _retention_tables()
    w_in_p = jnp.take(w_in, jnp.asarray(_IN_PERM), axis=2).astype(BF16)
    w_out_b = w_out.astype(BF16)
    wgu1, wd1 = ffn1_w_gu.astype(BF16), ffn1_w_down.astype(BF16)
    wgu2, wd2 = ffn2_w_gu.astype(BF16), ffn2_w_down.astype(BF16)

    def vec(a, l):
        return a[l].reshape(1, D_MODEL)

    h = x.reshape(t, D_MODEL)
    for l in range(DEPTH):
        h = _ffn(h, wgu1[l], wd1[l], vec(ln1_g, l), vec(ln1_b, l), tm=tm, tf=256)
        h = _mixer(h, attn_sinks[l], w_in_p[l], w_out_b[l], vec(ln2_g, l), vec(ln2_b, l),
                   rot_tables, ret_tables, batch=batch, seq=seq, ts=ts)
        h = _ffn(h, wgu2[l], wd2[l], vec(ln3_g, l), vec(ln3_b, l), tm=tm, tf=256)
    return h.reshape(batch, seq, D_MODEL)
```

```python
import functools

import numpy as np
import jax
import jax.numpy as jnp
from jax import lax
from jax.experimental import pallas as pl
from jax.experimental.pallas import tpu as pltpu

F32 = jnp.float32
BF16 = jnp.bfloat16

D_MODEL = 1024
DEPTH = 4
HEAD_DIM = 64
HALF = HEAD_DIM // 2
ATTN_Q_HEADS = 8
ATTN_KV_HEADS = 2
RET_HEADS = 8
ATTN_WIDTH = ATTN_Q_HEADS * HEAD_DIM
KV_WIDTH = ATTN_KV_HEADS * HEAD_DIM
RET_WIDTH = RET_HEADS * HEAD_DIM
WINDOW = 128
CHUNK = 128
ROPE_THETA = 10000.0
RET_THETA = 10000.0
D_FF = 2816
LN_EPS = 1e-5
GN_EPS = 1e-6
ALPHA = (2 * DEPTH) ** 0.25
QK_SCALE = HEAD_DIM ** -0.5

LANES = 128
PAIR = 2 * HEAD_DIM
N_PAIRS = RET_HEADS // 2

QA0, KA0, VA0 = 0, 512, 768
QR0, KR0, VR0, GR0 = 1024, 1536, 2048, 2560
IN_COLS = 3072

VMEM_LIMIT_BYTES = 56 * 1024 * 1024


def _in_proj_column_permutation():
    first = np.arange(HALF)
    cols = []
    for p in range(ATTN_Q_HEADS // 2):
        a, b = 2 * p * HEAD_DIM, (2 * p + 1) * HEAD_DIM
        cols += [a + first, b + first, a + HALF + first, b + HALF + first]
    k0 = ATTN_WIDTH
    for g in range(ATTN_KV_HEADS):
        a = k0 + g * HEAD_DIM
        cols += [a + first, a + first, a + HALF + first, a + HALF + first]
    v0 = ATTN_WIDTH + KV_WIDTH
    for g in range(ATTN_KV_HEADS):
        a = v0 + g * HEAD_DIM + np.arange(HEAD_DIM)
        cols += [a, a]
    for base in (ATTN_WIDTH + 2 * KV_WIDTH, ATTN_WIDTH + 2 * KV_WIDTH + RET_WIDTH):
        for p in range(N_PAIRS):
            a, b = base + 2 * p * HEAD_DIM, base + (2 * p + 1) * HEAD_DIM
            cols += [a + 2 * first, b + 2 * first, a + 2 * first + 1, b + 2 * first + 1]
    vr0 = ATTN_WIDTH + 2 * KV_WIDTH + 2 * RET_WIDTH
    cols += [vr0 + np.arange(2 * RET_WIDTH)]
    perm = np.concatenate(cols).astype(np.int32)
    assert perm.shape == (IN_COLS,)
    return perm


_IN_PERM = _in_proj_column_permutation()


def _rotation_tables(seq):
    pos = jnp.arange(seq, dtype=F32)
    inv_freq = ROPE_THETA ** (-jnp.arange(0, HEAD_DIM, 2, dtype=F32) / HEAD_DIM)
    ang = pos[:, None] * inv_freq[None, :]
    ret_freq = 1.0 / (RET_THETA ** jnp.linspace(0.0, 1.0, HEAD_DIM // 2, dtype=F32))
    rang = pos[:, None] * ret_freq[None, :]

    def lay(c, s):
        return (jnp.concatenate([c, c, c, c], axis=1),
                jnp.concatenate([-s, -s, s, s], axis=1))

    return lay(jnp.cos(ang), jnp.sin(ang)) + lay(jnp.cos(rang), jnp.sin(rang))


def _retention_tables():
    h = jnp.arange(RET_HEADS, dtype=F32)
    log_gamma = jnp.log1p(-jnp.exp2(-5.0 - h))
    idx = jnp.arange(CHUNK, dtype=F32)
    diff = idx[:, None] - idx[None, :]
    decay = jnp.where(diff[None] >= 0,
                      jnp.exp(jnp.maximum(diff, 0.0)[None] * log_gamma[:, None, None]), 0.0)
    decay = jnp.transpose(decay, (1, 0, 2)).reshape(CHUNK, RET_HEADS * CHUNK)
    w_k = jnp.exp((CHUNK - 1.0 - idx)[None, :] * log_gamma[:, None])
    w_q = jnp.exp((idx + 1.0)[None, :] * log_gamma[:, None])
    g_chunk = jnp.exp(CHUNK * log_gamma)

    def rolled(w):
        w = w.T.reshape(CHUNK, N_PAIRS, 2, 1)
        return jnp.broadcast_to(jnp.concatenate([w, w], axis=2), (CHUNK, N_PAIRS, 4, HALF)).reshape(CHUNK, RET_WIDTH)

    g_nat = jnp.repeat(g_chunk, HEAD_DIM)[None, :]
    return decay, rolled(w_q), rolled(w_k), g_nat


def _layer_norm(z, g, b):
    mu = jnp.mean(z, axis=-1, keepdims=True)
    d = z - mu
    var = jnp.mean(d * d, axis=-1, keepdims=True)
    return d * lax.rsqrt(var + LN_EPS) * g + b


def _dot(a, b):
    return jnp.dot(a, b, preferred_element_type=F32)


def _dot_nt(a, b):
    return lax.dot_general(a, b, (((1,), (1,)), ((), ())), preferred_element_type=F32)


def _dot_tn(a, b):
    return lax.dot_general(a, b, (((0,), (0,)), ((), ())), preferred_element_type=F32)


def _ffn_kernel(x_ref, wgu_ref, wd_ref, g_ref, b_ref, o_ref, h_scr, *, tf):
    x = x_ref[...]
    xb = x.astype(BF16)
    for c in range(D_FF // tf):
        a = _dot(xb, wgu_ref[:, c * tf:(c + 1) * tf])
        u = _dot(xb, wgu_ref[:, D_FF + c * tf:D_FF + (c + 1) * tf])
        h_scr[:, c * tf:(c + 1) * tf] = (jax.nn.silu(a) * u).astype(BF16)
    y = _dot(h_scr[...], wd_ref[...])
    o_ref[...] = _layer_norm(ALPHA * x + 0.5 * y, g_ref[...], b_ref[...])


def _resident(shape):
    return pl.BlockSpec(shape, lambda *_: (0,) * len(shape), pipeline_mode=pl.Buffered(1))


def _ffn(x, wgu, wd, g, b, *, tm, tf):
    t = x.shape[0]
    return pl.pallas_call(
        functools.partial(_ffn_kernel, tf=tf),
        grid=(t // tm,),
        in_specs=[pl.BlockSpec((tm, D_MODEL), lambda i: (i, 0)),
                  _resident((D_MODEL, 2 * D_FF)),
                  _resident((D_FF, D_MODEL)),
                  _resident((1, D_MODEL)),
                  _resident((1, D_MODEL))],
        out_specs=pl.BlockSpec((tm, D_MODEL), lambda i: (i, 0)),
        out_shape=jax.ShapeDtypeStruct((t, D_MODEL), F32),
        scratch_shapes=[pltpu.VMEM((tm, D_FF), BF16)],
        compiler_params=pltpu.CompilerParams(dimension_semantics=("arbitrary",),
                                             vmem_limit_bytes=VMEM_LIMIT_BYTES),
        name="ffn",
    )(x, wgu, wd, g, b)


def _rotate(h, cos, sin):
    outs = []
    for j in range(h.shape[1] // LANES):
        hj = h[:, j * LANES:(j + 1) * LANES]
        outs.append(hj * cos + pltpu.roll(hj, PAIR // 2, 1) * sin)
    return jnp.concatenate(outs, axis=1)


def _mixer_kernel(sinks_ref, x_ref, win_ref, wout_ref, g_ref, b_ref,
                  rcos_ref, rsin_ref, tcos_ref, tsin_ref,
                  decay_ref, wq_ref, wk_ref, gch_ref,
                  o_ref,
                  qa_s, ka_s, va_s, qr_s, kr_s, vr_s, gr_s, y_s, state_s, *, ts):
    s_idx = pl.program_id(1)
    n_chunks = ts // CHUNK

    @pl.when(s_idx == 0)
    def _():
        ka_s[0:CHUNK, :] = jnp.zeros((CHUNK, 2 * PAIR), BF16)
        va_s[0:CHUNK, :] = jnp.zeros((CHUNK, 2 * PAIR), BF16)
        state_s[...] = jnp.zeros(state_s.shape, F32)

    x = x_ref[...]
    xb = x.astype(BF16)
    rcos, rsin = rcos_ref[...], rsin_ref[...]
    tcos, tsin = tcos_ref[...], tsin_ref[...]
    h = _dot(xb, win_ref[:, QA0:QA0 + ATTN_WIDTH])
    qa_s[...] = (_rotate(h, rcos, rsin) * QK_SCALE).astype(BF16)
    h = _dot(xb, win_ref[:, KA0:KA0 + 4 * PAIR])
    ka_s[CHUNK:, :] = _rotate(h[:, :2 * PAIR], rcos, rsin).astype(BF16)
    va_s[CHUNK:, :] = h[:, 2 * PAIR:].astype(BF16)
    h = _dot(xb, win_ref[:, QR0:QR0 + RET_WIDTH])
    qr_s[...] = _rotate(h, tcos, tsin)
    h = _dot(xb, win_ref[:, KR0:KR0 + RET_WIDTH])
    kr_s[...] = _rotate(h, tcos, tsin) * QK_SCALE
    h = _dot(xb, win_ref[:, VR0:VR0 + RET_WIDTH])
    vr_s[...] = h.astype(BF16)
    gr_s[...] = _dot(xb, win_ref[:, GR0:GR0 + RET_WIDTH])

    lane = lax.broadcasted_iota(jnp.int32, (1, LANES), 1)
    slot_a = (lane % HEAD_DIM) < HALF
    lo = lane < HEAD_DIM
    row = lax.broadcasted_iota(jnp.int32, (LANES, LANES), 0)
    col = lax.broadcasted_iota(jnp.int32, (LANES, LANES), 1)
    state_mask = ((row % HEAD_DIM) < HALF) == (col < HEAD_DIM)
    gn_avg = jnp.where((row < HEAD_DIM) == (col < HEAD_DIM), 1.0 / HEAD_DIM, 0.0).astype(BF16)
    qi = lax.broadcasted_iota(jnp.int32, (CHUNK, 2 * CHUNK), 0)
    kk = lax.broadcasted_iota(jnp.int32, (CHUNK, 2 * CHUNK), 1)
    band = (kk > qi) & (kk <= qi + WINDOW)

    gn_avg2 = jnp.concatenate([gn_avg, gn_avg], axis=0)

    def half_means(v):
        hi = v.astype(BF16)
        lo_part = (v - hi.astype(F32)).astype(BF16)
        return _dot(jnp.concatenate([hi, lo_part], axis=1), gn_avg2)

    def chunk_body(c, carry):
        r0 = pl.multiple_of(c * CHUNK, CHUNK)
        rows = pl.ds(r0, CHUNK)
        kv_rows = pl.ds(r0, 2 * CHUNK)
        has_prev = (s_idx * n_chunks + c) > 0
        mask = band & ((kk >= CHUNK) | has_prev)


        s_att, v_att = [], []
        for g in range(ATTN_KV_HEADS):
            kd = ka_s[kv_rows, g * PAIR:(g + 1) * PAIR]
            vd = va_s[kv_rows, g * PAIR:(g + 1) * PAIR]
            zero = jnp.zeros_like(kd)
            k_cat = jnp.concatenate([jnp.where(slot_a, kd, zero), jnp.where(slot_a, zero, kd)], axis=0)
            v_att.append(jnp.concatenate([jnp.where(lo, vd, zero), jnp.where(lo, zero, vd)], axis=0))
            q_st = jnp.concatenate([qa_s[rows, (2 * g) * PAIR:(2 * g + 1) * PAIR],
                                    qa_s[rows, (2 * g + 1) * PAIR:(2 * g + 2) * PAIR]], axis=0)
            s_att.append(_dot_nt(q_st, k_cat))

        s_ret, rhs_ret, qw_ret = [], [], []
        for p in range(N_PAIRS):
            cols = slice(p * PAIR, (p + 1) * PAIR)
            q = qr_s[rows, cols]
            k = kr_s[rows, cols]
            v = vr_s[rows, cols]
            kb = k.astype(BF16)
            zero = jnp.zeros_like(kb)
            k_cat = jnp.concatenate([jnp.where(slot_a, kb, zero), jnp.where(slot_a, zero, kb)], axis=0)
            s_ret.append(_dot_nt(q.astype(BF16), k_cat))
            st = state_s[p]
            rhs_ret.append(jnp.concatenate([jnp.where(lo, v, zero), jnp.where(lo, zero, v),
                                            st.astype(BF16)], axis=0))
            qw_ret.append((q * wq_ref[:, cols]).astype(BF16))
            kv = _dot_tn((k * wk_ref[:, cols]).astype(BF16), v)
            state_s[p] = st * gch_ref[:, cols] + jnp.where(state_mask, kv, 0.0)

        for g in range(ATTN_KV_HEADS):
            p_rows = []
            for pp in range(2):
                p_cols = []
                for hh in range(2):
                    sink = sinks_ref[2 * (2 * g + pp) + hh]
                    s = s_att[g][pp * CHUNK:(pp + 1) * CHUNK, hh * 2 * CHUNK:(hh + 1) * 2 * CHUNK]
                    s = jnp.where(mask, s, -jnp.inf)
                    m = jnp.maximum(jnp.max(s, axis=-1, keepdims=True), sink)
                    e = jnp.exp(s - m)
                    denom = jnp.sum(e, axis=-1, keepdims=True) + jnp.exp(sink - m)
                    p_cols.append((e / denom).astype(BF16))
                p_rows.append(jnp.concatenate(p_cols, axis=1))
            o = _dot(jnp.concatenate(p_rows, axis=0), v_att[g])
            y_s[rows, (2 * g) * PAIR:(2 * g + 1) * PAIR] = o[:CHUNK].astype(BF16)
            y_s[rows, (2 * g + 1) * PAIR:(2 * g + 2) * PAIR] = o[CHUNK:].astype(BF16)

        o_ret = []
        for p in range(N_PAIRS):
            sd = (s_ret[p] * decay_ref[:, (2 * p) * CHUNK:(2 * p + 2) * CHUNK]).astype(BF16)
            o_ret.append(_dot(jnp.concatenate([sd, qw_ret[p]], axis=1), rhs_ret[p]))

        o_all = jnp.concatenate(o_ret, axis=0)
        d = o_all - half_means(o_all)
        on = d * lax.rsqrt(half_means(d * d) + GN_EPS)
        for p in range(N_PAIRS):
            cols = slice(p * PAIR, (p + 1) * PAIR)
            y_s[rows, ATTN_WIDTH + p * PAIR:ATTN_WIDTH + (p + 1) * PAIR] = (
                jax.nn.silu(gr_s[rows, cols]) * on[p * CHUNK:(p + 1) * CHUNK]).astype(BF16)
        return carry

    lax.fori_loop(0, n_chunks, chunk_body, 0)

    ka_s[0:CHUNK, :] = ka_s[ts:ts + CHUNK, :]
    va_s[0:CHUNK, :] = va_s[ts:ts + CHUNK, :]

    y = _dot(y_s[...], wout_ref[...])
    o_ref[...] = _layer_norm(ALPHA * x + y, g_ref[...], b_ref[...])


def _mixer(x, sinks, win, wout, g, b, rot_tables, ret_tables, *, batch, seq, ts):
    n_s = seq // ts
    rcos, rsin, tcos, tsin = rot_tables
    decay, wq, wk, gch = ret_tables
    row_spec = pl.BlockSpec((ts, D_MODEL), lambda bi, si: (bi * n_s + si, 0))
    tab_spec = pl.BlockSpec((ts, LANES), lambda bi, si: (si, 0))
    return pl.pallas_call(
        functools.partial(_mixer_kernel, ts=ts),
        grid=(batch, n_s),
        in_specs=[pl.BlockSpec(memory_space=pltpu.SMEM),
                  row_spec,
                  _resident((D_MODEL, IN_COLS)),
                  _resident((ATTN_WIDTH + RET_WIDTH, D_MODEL)),
                  _resident((1, D_MODEL)),
                  _resident((1, D_MODEL)),
                  tab_spec, tab_spec, tab_spec, tab_spec,
                  _resident((CHUNK, RET_HEADS * CHUNK)),
                  _resident((CHUNK, RET_WIDTH)),
                  _resident((CHUNK, RET_WIDTH)),
                  _resident((1, RET_WIDTH))],
        out_specs=row_spec,
        out_shape=jax.ShapeDtypeStruct((batch * seq, D_MODEL), F32),
        scratch_shapes=[pltpu.VMEM((ts, ATTN_WIDTH), BF16),
                        pltpu.VMEM((CHUNK + ts, 2 * PAIR), BF16),
                        pltpu.VMEM((CHUNK + ts, 2 * PAIR), BF16),
                        pltpu.VMEM((ts, RET_WIDTH), F32),
                        pltpu.VMEM((ts, RET_WIDTH), F32),
                        pltpu.VMEM((ts, RET_WIDTH), BF16),
                        pltpu.VMEM((ts, RET_WIDTH), F32),
                        pltpu.VMEM((ts, ATTN_WIDTH + RET_WIDTH), BF16),
                        pltpu.VMEM((N_PAIRS, LANES, LANES), F32)],
        compiler_params=pltpu.CompilerParams(dimension_semantics=("arbitrary", "arbitrary"),
                                             vmem_limit_bytes=VMEM_LIMIT_BYTES),
        name="mixer",
    )(sinks, x, win, wout, g, b, rcos, rsin, tcos, tsin, decay, wq, wk, gch)


def _pick_tile(n, want):
    t = min(n, want)
    assert n % t == 0 and t % CHUNK == 0, (n, t)
    return t


def kernel(x, w_in, w_out, attn_sinks, ffn1_w_gu, ffn1_w_down, ffn2_w_gu, ffn2_w_down,
           ln1_g, ln1_b, ln2_g, ln2_b, ln3_g, ln3_b):
    batch, seq, d = x.shape
    assert d == D_MODEL and seq % CHUNK == 0
    assert w_in.shape[0] == DEPTH
    t = batch * seq
    tm = _pick_tile(t, 512)
    ts = _pick_tile(seq, 512)

    rot_tables = _rotation_tables(seq)
    ret_tables = _retention_tables()
    w_in_p = jnp.take(w_in, jnp.asarray(_IN_PERM), axis=2).astype(BF16)
    w_out_b = w_out.astype(BF16)
    wgu1, wd1 = ffn1_w_gu.astype(BF16), ffn1_w_down.astype(BF16)
    wgu2, wd2 = ffn2_w_gu.astype(BF16), ffn2_w_down.astype(BF16)

    def vec(a, l):
        return a[l].reshape(1, D_MODEL)

    h = x.reshape(t, D_MODEL)
    for l in range(DEPTH):
        h = _ffn(h, wgu1[l], wd1[l], vec(ln1_g, l), vec(ln1_b, l), tm=tm, tf=256)
        h = _mixer(h, attn_sinks[l], w_in_p[l], w_out_b[l], vec(ln2_g, l), vec(ln2_b, l),
                   rot_tables, ret_tables, batch=batch, seq=seq, ts=ts)
        h = _ffn(h, wgu2[l], wd2[l], vec(ln3_g, l), vec(ln3_b, l), tm=tm, tf=256)
    return h.reshape(batch, seq, D_MODEL)
```

```python
import functools

import numpy as np
import jax
import jax.numpy as jnp
from jax import lax
from jax.experimental import pallas as pl
from jax.experimental.pallas import tpu as pltpu

F32 = jnp.float32
BF16 = jnp.bfloat16

D_MODEL = 1024
DEPTH = 4
HEAD_DIM = 64
HALF = HEAD_DIM // 2
ATTN_Q_HEADS = 8
ATTN_KV_HEADS = 2
RET_HEADS = 8
ATTN_WIDTH = ATTN_Q_HEADS * HEAD_DIM
KV_WIDTH = ATTN_KV_HEADS * HEAD_DIM
RET_WIDTH = RET_HEADS * HEAD_DIM
WINDOW = 128
CHUNK = 128
ROPE_THETA = 10000.0
RET_THETA = 10000.0
D_FF = 2816
LN_EPS = 1e-5
GN_EPS = 1e-6
ALPHA = (2 * DEPTH) ** 0.25
QK_SCALE = HEAD_DIM ** -0.5

LANES = 128
PAIR = 2 * HEAD_DIM
N_PAIRS = RET_HEADS // 2

QA0, KA0, VA0 = 0, 512, 768
QR0, KR0, VR0, GR0 = 1024, 1536, 2048, 2560
IN_COLS = 3072

VMEM_LIMIT_BYTES = 56 * 1024 * 1024


def _in_proj_column_permutation():
    first = np.arange(HALF)
    cols = []
    for p in range(ATTN_Q_HEADS // 2):
        a, b = 2 * p * HEAD_DIM, (2 * p + 1) * HEAD_DIM
        cols += [a + first, b + first, a + HALF + first, b + HALF + first]
    k0 = ATTN_WIDTH
    for g in range(ATTN_KV_HEADS):
        a = k0 + g * HEAD_DIM
        cols += [a + first, a + first, a + HALF + first, a + HALF + first]
    v0 = ATTN_WIDTH + KV_WIDTH
    for g in range(ATTN_KV_HEADS):
        a = v0 + g * HEAD_DIM + np.arange(HEAD_DIM)
        cols += [a, a]
    for base in (ATTN_WIDTH + 2 * KV_WIDTH, ATTN_WIDTH + 2 * KV_WIDTH + RET_WIDTH):
        for p in range(N_PAIRS):
            a, b = base + 2 * p * HEAD_DIM, base + (2 * p + 1) * HEAD_DIM
            cols += [a + 2 * first, b + 2 * first, a + 2 * first + 1, b + 2 * first + 1]
    vr0 = ATTN_WIDTH + 2 * KV_WIDTH + 2 * RET_WIDTH
    cols += [vr0 + np.arange(2 * RET_WIDTH)]
    perm = np.concatenate(cols).astype(np.int32)
    assert perm.shape == (IN_COLS,)
    return perm


_IN_PERM = _in_proj_column_permutation()


def _rotation_tables(seq):
    pos = jnp.arange(seq, dtype=F32)
    inv_freq = ROPE_THETA ** (-jnp.arange(0, HEAD_DIM, 2, dtype=F32) / HEAD_DIM)
    ang = pos[:, None] * inv_freq[None, :]
    ret_freq = 1.0 / (RET_THETA ** jnp.linspace(0.0, 1.0, HEAD_DIM // 2, dtype=F32))
    rang = pos[:, None] * ret_freq[None, :]

    def lay(c, s):
        return (jnp.concatenate([c, c, c, c], axis=1),
                jnp.concatenate([-s, -s, s, s], axis=1))

    return lay(jnp.cos(ang), jnp.sin(ang)) + lay(jnp.cos(rang), jnp.sin(rang))


def _retention_tables():
    h = jnp.arange(RET_HEADS, dtype=F32)
    log_gamma = jnp.log1p(-jnp.exp2(-5.0 - h))
    idx = jnp.arange(CHUNK, dtype=F32)
    diff = idx[:, None] - idx[None, :]
    decay = jnp.where(diff[None] >= 0,
                      jnp.exp(jnp.maximum(diff, 0.0)[None] * log_gamma[:, None, None]), 0.0)
    decay = jnp.transpose(decay, (1, 0, 2)).reshape(CHUNK, RET_HEADS * CHUNK)
    w_k = jnp.exp((CHUNK - 1.0 - idx)[None, :] * log_gamma[:, None])
    w_q = jnp.exp((idx + 1.0)[None, :] * log_gamma[:, None])
    g_chunk = jnp.exp(CHUNK * log_gamma)

    def rolled(w):
        w = w.T.reshape(CHUNK, N_PAIRS, 2, 1)
        return jnp.broadcast_to(jnp.concatenate([w, w], axis=2), (CHUNK, N_PAIRS, 4, HALF)).reshape(CHUNK, RET_WIDTH)

    g_nat = jnp.repeat(g_chunk, HEAD_DIM)[None, :]
    return decay, rolled(w_q), rolled(w_k), g_nat


def _layer_norm(z, g, b):
    mu = jnp.mean(z, axis=-1, keepdims=True)
    d = z - mu
    var = jnp.mean(d * d, axis=-1, keepdims=True)
    return d * lax.rsqrt(var + LN_EPS) * g + b


def _dot(a, b):
    return jnp.dot(a, b, preferred_element_type=F32)


def _dot_nt(a, b):
    return lax.dot_general(a, b, (((1,), (1,)), ((), ())), preferred_element_type=F32)


def _dot_tn(a, b):
    return lax.dot_general(a, b, (((0,), (0,)), ((), ())), preferred_element_type=F32)


def _ffn_kernel(x_ref, wgu_ref, wd_ref, g_ref, b_ref, o_ref, h_scr, *, tf, sub):
    for r in range(x_ref.shape[0] // sub):
        rs = slice(r * sub, (r + 1) * sub)
        x = x_ref[rs, :]
        xb = x.astype(BF16)
        for c in range(D_FF // tf):
            a = _dot(xb, wgu_ref[:, c * tf:(c + 1) * tf])
            u = _dot(xb, wgu_ref[:, D_FF + c * tf:D_FF + (c + 1) * tf])
            h_scr[rs, c * tf:(c + 1) * tf] = (jax.nn.silu(a) * u).astype(BF16)
        y = _dot(h_scr[rs, :], wd_ref[...])
        o_ref[rs, :] = _layer_norm(ALPHA * x + 0.5 * y, g_ref[...], b_ref[...])


def _resident(shape):
    return pl.BlockSpec(shape, lambda *_: (0,) * len(shape), pipeline_mode=pl.Buffered(1))


def _layer_resident(layer, shape):
    return pl.BlockSpec((None,) + shape, lambda *_: (layer,) + (0,) * len(shape),
                        pipeline_mode=pl.Buffered(1))


def _ffn(x, wgu, wd, g, b, layer, *, tm, tf, sub):
    t = x.shape[0]
    return pl.pallas_call(
        functools.partial(_ffn_kernel, tf=tf, sub=sub),
        grid=(t // tm,),
        in_specs=[pl.BlockSpec((tm, D_MODEL), lambda i: (i, 0)),
                  _layer_resident(layer, (D_MODEL, 2 * D_FF)),
                  _layer_resident(layer, (D_FF, D_MODEL)),
                  _layer_resident(layer, (1, D_MODEL)),
                  _layer_resident(layer, (1, D_MODEL))],
        out_specs=pl.BlockSpec((tm, D_MODEL), lambda i: (i, 0)),
        out_shape=jax.ShapeDtypeStruct((t, D_MODEL), F32),
        scratch_shapes=[pltpu.VMEM((tm, D_FF), BF16)],
        compiler_params=pltpu.CompilerParams(dimension_semantics=("arbitrary",),
                                             vmem_limit_bytes=VMEM_LIMIT_BYTES),
        name="ffn",
    )(x, wgu, wd, g, b)


def _rotate(h, cos, sin):
    outs = []
    for j in range(h.shape[1] // LANES):
        hj = h[:, j * LANES:(j + 1) * LANES]
        outs.append(hj * cos + pltpu.roll(hj, PAIR // 2, 1) * sin)
    return jnp.concatenate(outs, axis=1)


def _mixer_kernel(sinks_ref, x_ref, win_ref, wout_ref, g_ref, b_ref,
                  rcos_ref, rsin_ref, tcos_ref, tsin_ref,
                  decay_ref, wq_ref, wk_ref, gch_ref,
                  o_ref,
                  qa_s, ka_s, va_s, qr_s, kr_s, vr_s, gr_s, y_s, state_s, *, ts):
    s_idx = pl.program_id(1)
    n_chunks = ts // CHUNK

    @pl.when(s_idx == 0)
    def _():
        ka_s[0:CHUNK, :] = jnp.zeros((CHUNK, 2 * PAIR), BF16)
        va_s[0:CHUNK, :] = jnp.zeros((CHUNK, 2 * PAIR), BF16)
        state_s[...] = jnp.zeros(state_s.shape, F32)

    x = x_ref[...]
    xb = x.astype(BF16)
    rcos, rsin = rcos_ref[...], rsin_ref[...]
    tcos, tsin = tcos_ref[...], tsin_ref[...]
    h = _dot(xb, win_ref[:, QA0:QA0 + ATTN_WIDTH])
    qa_s[...] = (_rotate(h, rcos, rsin) * QK_SCALE).astype(BF16)
    h = _dot(xb, win_ref[:, KA0:KA0 + 4 * PAIR])
    ka_s[CHUNK:, :] = _rotate(h[:, :2 * PAIR], rcos, rsin).astype(BF16)
    va_s[CHUNK:, :] = h[:, 2 * PAIR:].astype(BF16)
    h = _dot(xb, win_ref[:, QR0:QR0 + RET_WIDTH])
    qr_s[...] = _rotate(h, tcos, tsin)
    h = _dot(xb, win_ref[:, KR0:KR0 + RET_WIDTH])
    kr_s[...] = _rotate(h, tcos, tsin) * QK_SCALE
    h = _dot(xb, win_ref[:, VR0:VR0 + RET_WIDTH])
    vr_s[...] = h.astype(BF16)
    gr_s[...] = _dot(xb, win_ref[:, GR0:GR0 + RET_WIDTH])

    lane = lax.broadcasted_iota(jnp.int32, (1, LANES), 1)
    slot_a = (lane % HEAD_DIM) < HALF
    lo = lane < HEAD_DIM
    row = lax.broadcasted_iota(jnp.int32, (LANES, LANES), 0)
    col = lax.broadcasted_iota(jnp.int32, (LANES, LANES), 1)
    state_mask = ((row % HEAD_DIM) < HALF) == (col < HEAD_DIM)
    gn_avg = jnp.where((row < HEAD_DIM) == (col < HEAD_DIM), 1.0 / HEAD_DIM, 0.0).astype(BF16)
    qi = lax.broadcasted_iota(jnp.int32, (CHUNK, 2 * CHUNK), 0)
    kk = lax.broadcasted_iota(jnp.int32, (CHUNK, 2 * CHUNK), 1)
    band = (kk > qi) & (kk <= qi + WINDOW)

    gn_avg2 = jnp.concatenate([gn_avg, gn_avg], axis=0)

    def half_means(v):
        hi = v.astype(BF16)
        lo_part = (v - hi.astype(F32)).astype(BF16)
        return _dot(jnp.concatenate([hi, lo_part], axis=1), gn_avg2)

    def chunk_body(c, carry):
        r0 = pl.multiple_of(c * CHUNK, CHUNK)
        rows = pl.ds(r0, CHUNK)
        kv_rows = pl.ds(r0, 2 * CHUNK)
        has_prev = (s_idx * n_chunks + c) > 0
        mask = band & ((kk >= CHUNK) | has_prev)


        s_att, v_att = [], []
        for g in range(ATTN_KV_HEADS):
            kd = ka_s[kv_rows, g * PAIR:(g + 1) * PAIR]
            vd = va_s[kv_rows, g * PAIR:(g + 1) * PAIR]
            zero = jnp.zeros_like(kd)
            k_cat = jnp.concatenate([jnp.where(slot_a, kd, zero), jnp.where(slot_a, zero, kd)], axis=0)
            v_att.append(jnp.concatenate([jnp.where(lo, vd, zero), jnp.where(lo, zero, vd)], axis=0))
            q_st = jnp.concatenate([qa_s[rows, (2 * g) * PAIR:(2 * g + 1) * PAIR],
                                    qa_s[rows, (2 * g + 1) * PAIR:(2 * g + 2) * PAIR]], axis=0)
            s_att.append(_dot_nt(q_st, k_cat))

        s_ret, rhs_ret, qw_ret = [], [], []
        for p in range(N_PAIRS):
            cols = slice(p * PAIR, (p + 1) * PAIR)
            q = qr_s[rows, cols]
            k = kr_s[rows, cols]
            v = vr_s[rows, cols]
            kb = k.astype(BF16)
            zero = jnp.zeros_like(kb)
            k_cat = jnp.concatenate([jnp.where(slot_a, kb, zero), jnp.where(slot_a, zero, kb)], axis=0)
            s_ret.append(_dot_nt(q.astype(BF16), k_cat))
            st = state_s[p]
            rhs_ret.append(jnp.concatenate([jnp.where(lo, v, zero), jnp.where(lo, zero, v),
                                            st.astype(BF16)], axis=0))
            qw_ret.append((q * wq_ref[:, cols]).astype(BF16))
            kv = _dot_tn((k * wk_ref[:, cols]).astype(BF16), v)
            state_s[p] = st * gch_ref[:, cols] + jnp.where(state_mask, kv, 0.0)

        for g in range(ATTN_KV_HEADS):
            p_rows = []
            for pp in range(2):
                p_cols = []
                for hh in range(2):
                    sink = sinks_ref[2 * (2 * g + pp) + hh]
                    s = s_att[g][pp * CHUNK:(pp + 1) * CHUNK, hh * 2 * CHUNK:(hh + 1) * 2 * CHUNK]
                    s = jnp.where(mask, s, -jnp.inf)
                    m = jnp.maximum(jnp.max(s, axis=-1, keepdims=True), sink)
                    e = jnp.exp(s - m)
                    denom = jnp.sum(e, axis=-1, keepdims=True) + jnp.exp(sink - m)
                    p_cols.append((e / denom).astype(BF16))
                p_rows.append(jnp.concatenate(p_cols, axis=1))
            o = _dot(jnp.concatenate(p_rows, axis=0), v_att[g])
            y_s[rows, (2 * g) * PAIR:(2 * g + 1) * PAIR] = o[:CHUNK].astype(BF16)
            y_s[rows, (2 * g + 1) * PAIR:(2 * g + 2) * PAIR] = o[CHUNK:].astype(BF16)

        o_ret = []
        for p in range(N_PAIRS):
            sd = (s_ret[p] * decay_ref[:, (2 * p) * CHUNK:(2 * p + 2) * CHUNK]).astype(BF16)
            o_ret.append(_dot(jnp.concatenate([sd, qw_ret[p]], axis=1), rhs_ret[p]))

        o_all = jnp.concatenate(o_ret, axis=0)
        d = o_all - half_means(o_all)
        on = d * lax.rsqrt(half_means(d * d) + GN_EPS)
        for p in range(N_PAIRS):
            cols = slice(p * PAIR, (p + 1) * PAIR)
            y_s[rows, ATTN_WIDTH + p * PAIR:ATTN_WIDTH + (p + 1) * PAIR] = (
                jax.nn.silu(gr_s[rows, cols]) * on[p * CHUNK:(p + 1) * CHUNK]).astype(BF16)
        return carry

    lax.fori_loop(0, n_chunks, chunk_body, 0)

    ka_s[0:CHUNK, :] = ka_s[ts:ts + CHUNK, :]
    va_s[0:CHUNK, :] = va_s[ts:ts + CHUNK, :]

    y = _dot(y_s[...], wout_ref[...])
    o_ref[...] = _layer_norm(ALPHA * x + y, g_ref[...], b_ref[...])


def _mixer(x, sinks, win, wout, g, b, rot_tables, ret_tables, layer, *, batch, seq, ts):
    n_s = seq // ts
    rcos, rsin, tcos, tsin = rot_tables
    decay, wq, wk, gch = ret_tables
    row_spec = pl.BlockSpec((ts, D_MODEL), lambda bi, si: (bi * n_s + si, 0))
    tab_spec = pl.BlockSpec((ts, LANES), lambda bi, si: (si, 0))
    return pl.pallas_call(
        functools.partial(_mixer_kernel, ts=ts),
        grid=(batch, n_s),
        in_specs=[pl.BlockSpec(memory_space=pltpu.SMEM),
                  row_spec,
                  _layer_resident(layer, (D_MODEL, IN_COLS)),
                  _layer_resident(layer, (ATTN_WIDTH + RET_WIDTH, D_MODEL)),
                  _layer_resident(layer, (1, D_MODEL)),
                  _layer_resident(layer, (1, D_MODEL)),
                  tab_spec, tab_spec, tab_spec, tab_spec,
                  _resident((CHUNK, RET_HEADS * CHUNK)),
                  _resident((CHUNK, RET_WIDTH)),
                  _resident((CHUNK, RET_WIDTH)),
                  _resident((1, RET_WIDTH))],
        out_specs=row_spec,
        out_shape=jax.ShapeDtypeStruct((batch * seq, D_MODEL), F32),
        scratch_shapes=[pltpu.VMEM((ts, ATTN_WIDTH), BF16),
                        pltpu.VMEM((CHUNK + ts, 2 * PAIR), BF16),
                        pltpu.VMEM((CHUNK + ts, 2 * PAIR), BF16),
                        pltpu.VMEM((ts, RET_WIDTH), F32),
                        pltpu.VMEM((ts, RET_WIDTH), F32),
                        pltpu.VMEM((ts, RET_WIDTH), BF16),
                        pltpu.VMEM((ts, RET_WIDTH), F32),
                        pltpu.VMEM((ts, ATTN_WIDTH + RET_WIDTH), BF16),
                        pltpu.VMEM((N_PAIRS, LANES, LANES), F32)],
        compiler_params=pltpu.CompilerParams(dimension_semantics=("arbitrary", "arbitrary"),
                                             vmem_limit_bytes=VMEM_LIMIT_BYTES),
        name="mixer",
    )(sinks, x, win, wout, g, b, rcos, rsin, tcos, tsin, decay, wq, wk, gch)


def _pick_tile(n, want):
    t = min(n, want)
    assert n % t == 0 and t % CHUNK == 0, (n, t)
    return t


def kernel(x, w_in, w_out, attn_sinks, ffn1_w_gu, ffn1_w_down, ffn2_w_gu, ffn2_w_down,
           ln1_g, ln1_b, ln2_g, ln2_b, ln3_g, ln3_b):
    batch, seq, d = x.shape
    assert d == D_MODEL and seq % CHUNK == 0
    assert w_in.shape[0] == DEPTH
    t = batch * seq
    tm = _pick_tile(t, 1024)
    ts = _pick_tile(seq, 512)

    rot_tables = _rotation_tables(seq)
    ret_tables = _retention_tables()
    w_in_p = jnp.take(w_in, jnp.asarray(_IN_PERM), axis=2).astype(BF16)
    w_out_b = w_out.astype(BF16)
    wgu1, wd1 = ffn1_w_gu.astype(BF16), ffn1_w_down.astype(BF16)
    wgu2, wd2 = ffn2_w_gu.astype(BF16), ffn2_w_down.astype(BF16)

    def vecs(a):
        return a.reshape(DEPTH, 1, D_MODEL)

    g1, b1, g2, b2, g3, b3 = (vecs(a) for a in (ln1_g, ln1_b, ln2_g, ln2_b, ln3_g, ln3_b))
    h = x.reshape(t, D_MODEL)
    for l in range(DEPTH):
        h = _ffn(h, wgu1, wd1, g1, b1, l, tm=tm, tf=256, sub=256)
        h = _mixer(h, attn_sinks[l], w_in_p, w_out_b, g2, b2, rot_tables, ret_tables, l,
                   batch=batch, seq=seq, ts=ts)
        h = _ffn(h, wgu2, wd2, g3, b3, l, tm=tm, tf=256, sub=256)
    return h.reshape(batch, seq, D_MODEL)
```

```python
import functools

import numpy as np
import jax
import jax.numpy as jnp
from jax import lax
from jax.experimental import pallas as pl
from jax.experimental.pallas import tpu as pltpu

F32 = jnp.float32
BF16 = jnp.bfloat16

D_MODEL = 1024
DEPTH = 4
HEAD_DIM = 64
HALF = HEAD_DIM // 2
ATTN_Q_HEADS = 8
ATTN_KV_HEADS = 2
RET_HEADS = 8
ATTN_WIDTH = ATTN_Q_HEADS * HEAD_DIM
KV_WIDTH = ATTN_KV_HEADS * HEAD_DIM
RET_WIDTH = RET_HEADS * HEAD_DIM
WINDOW = 128
CHUNK = 128
ROPE_THETA = 10000.0
RET_THETA = 10000.0
D_FF = 2816
LN_EPS = 1e-5
GN_EPS = 1e-6
ALPHA = (2 * DEPTH) ** 0.25
QK_SCALE = HEAD_DIM ** -0.5

LANES = 128
PAIR = 2 * HEAD_DIM
N_PAIRS = RET_HEADS // 2

QA0, KA0, VA0 = 0, 512, 768
QR0, KR0, VR0, GR0 = 1024, 1536, 2048, 2560
IN_COLS = 3072

VMEM_LIMIT_BYTES = 56 * 1024 * 1024


def _in_proj_column_permutation():
    first = np.arange(HALF)
    cols = []
    for p in range(ATTN_Q_HEADS // 2):
        a, b = 2 * p * HEAD_DIM, (2 * p + 1) * HEAD_DIM
        cols += [a + first, b + first, a + HALF + first, b + HALF + first]
    k0 = ATTN_WIDTH
    for g in range(ATTN_KV_HEADS):
        a = k0 + g * HEAD_DIM
        cols += [a + first, a + first, a + HALF + first, a + HALF + first]
    v0 = ATTN_WIDTH + KV_WIDTH
    for g in range(ATTN_KV_HEADS):
        a = v0 + g * HEAD_DIM + np.arange(HEAD_DIM)
        cols += [a, a]
    for base in (ATTN_WIDTH + 2 * KV_WIDTH, ATTN_WIDTH + 2 * KV_WIDTH + RET_WIDTH):
        for p in range(N_PAIRS):
            a, b = base + 2 * p * HEAD_DIM, base + (2 * p + 1) * HEAD_DIM
            cols += [a + 2 * first, b + 2 * first, a + 2 * first + 1, b + 2 * first + 1]
    vr0 = ATTN_WIDTH + 2 * KV_WIDTH + 2 * RET_WIDTH
    cols += [vr0 + np.arange(2 * RET_WIDTH)]
    perm = np.concatenate(cols).astype(np.int32)
    assert perm.shape == (IN_COLS,)
    return perm


_IN_PERM = _in_proj_column_permutation()


def _rotation_tables(seq):
    pos = jnp.arange(seq, dtype=F32)
    inv_freq = ROPE_THETA ** (-jnp.arange(0, HEAD_DIM, 2, dtype=F32) / HEAD_DIM)
    ang = pos[:, None] * inv_freq[None, :]
    ret_freq = 1.0 / (RET_THETA ** jnp.linspace(0.0, 1.0, HEAD_DIM // 2, dtype=F32))
    rang = pos[:, None] * ret_freq[None, :]

    def lay(c, s):
        return (jnp.concatenate([c, c, c, c], axis=1),
                jnp.concatenate([-s, -s, s, s], axis=1))

    return lay(jnp.cos(ang), jnp.sin(ang)) + lay(jnp.cos(rang), jnp.sin(rang))


def _retention_tables():
    h = jnp.arange(RET_HEADS, dtype=F32)
    log_gamma = jnp.log1p(-jnp.exp2(-5.0 - h))
    idx = jnp.arange(CHUNK, dtype=F32)
    diff = idx[:, None] - idx[None, :]
    decay = jnp.where(diff[None] >= 0,
                      jnp.exp(jnp.maximum(diff, 0.0)[None] * log_gamma[:, None, None]), 0.0)
    decay = jnp.transpose(decay, (1, 0, 2)).reshape(CHUNK, RET_HEADS * CHUNK)
    w_k = jnp.exp((CHUNK - 1.0 - idx)[None, :] * log_gamma[:, None])
    w_q = jnp.exp((idx + 1.0)[None, :] * log_gamma[:, None])
    g_chunk = jnp.exp(CHUNK * log_gamma)

    def rolled(w):
        w = w.T.reshape(CHUNK, N_PAIRS, 2, 1)
        return jnp.broadcast_to(jnp.concatenate([w, w], axis=2), (CHUNK, N_PAIRS, 4, HALF)).reshape(CHUNK, RET_WIDTH)

    g_nat = jnp.repeat(g_chunk, HEAD_DIM)[None, :]
    return decay, rolled(w_q), rolled(w_k), g_nat


def _layer_norm(z, g, b):
    mu = jnp.mean(z, axis=-1, keepdims=True)
    d = z - mu
    var = jnp.mean(d * d, axis=-1, keepdims=True)
    return d * lax.rsqrt(var + LN_EPS) * g + b


def _dot(a, b):
    return jnp.dot(a, b, preferred_element_type=F32)


def _dot_nt(a, b):
    return lax.dot_general(a, b, (((1,), (1,)), ((), ())), preferred_element_type=F32)


def _dot_tn(a, b):
    return lax.dot_general(a, b, (((0,), (0,)), ((), ())), preferred_element_type=F32)


def _ffn_kernel(layer_ref, x_ref, wgu_ref, wd_ref, g_ref, b_ref, o_ref, h_scr, *, tf, sub):
    for r in range(x_ref.shape[0] // sub):
        rs = slice(r * sub, (r + 1) * sub)
        x = x_ref[rs, :]
        xb = x.astype(BF16)
        for c in range(D_FF // tf):
            a = _dot(xb, wgu_ref[:, c * tf:(c + 1) * tf])
            u = _dot(xb, wgu_ref[:, D_FF + c * tf:D_FF + (c + 1) * tf])
            h_scr[rs, c * tf:(c + 1) * tf] = (jax.nn.silu(a) * u).astype(BF16)
        y = _dot(h_scr[rs, :], wd_ref[...])
        o_ref[rs, :] = _layer_norm(ALPHA * x + 0.5 * y, g_ref[...], b_ref[...])


def _resident(shape):
    return pl.BlockSpec(shape, lambda *_: (0,) * len(shape), pipeline_mode=pl.Buffered(1))


def _layer_resident(shape):
    return pl.BlockSpec((None,) + shape, lambda *a: (a[-1][0],) + (0,) * len(shape),
                        pipeline_mode=pl.Buffered(1))


def _ffn(layer, x, wgu, wd, g, b, *, tm, tf, sub):
    t = x.shape[0]
    return pl.pallas_call(
        functools.partial(_ffn_kernel, tf=tf, sub=sub),
        grid_spec=pltpu.PrefetchScalarGridSpec(
            num_scalar_prefetch=1,
            grid=(t // tm,),
            in_specs=[pl.BlockSpec((tm, D_MODEL), lambda i, l: (i, 0)),
                      _layer_resident((D_MODEL, 2 * D_FF)),
                      _layer_resident((D_FF, D_MODEL)),
                      _layer_resident((1, D_MODEL)),
                      _layer_resident((1, D_MODEL))],
            out_specs=pl.BlockSpec((tm, D_MODEL), lambda i, l: (i, 0)),
            scratch_shapes=[pltpu.VMEM((tm, D_FF), BF16)]),
        out_shape=jax.ShapeDtypeStruct((t, D_MODEL), F32),
        compiler_params=pltpu.CompilerParams(dimension_semantics=("arbitrary",),
                                             vmem_limit_bytes=VMEM_LIMIT_BYTES),
        name="ffn",
    )(layer, x, wgu, wd, g, b)


def _rotate(h, cos, sin):
    outs = []
    for j in range(h.shape[1] // LANES):
        hj = h[:, j * LANES:(j + 1) * LANES]
        outs.append(hj * cos + pltpu.roll(hj, PAIR // 2, 1) * sin)
    return jnp.concatenate(outs, axis=1)


def _mixer_kernel(layer_ref, sinks_ref, x_ref, win_ref, wout_ref, g_ref, b_ref,
                  rcos_ref, rsin_ref, tcos_ref, tsin_ref,
                  decay_ref, wq_ref, wk_ref, gch_ref,
                  o_ref,
                  qa_s, ka_s, va_s, qr_s, kr_s, vr_s, gr_s, y_s, z_s, state_s, *, ts):
    s_idx = pl.program_id(1)
    n_chunks = ts // CHUNK
    assert n_chunks == 4

    @pl.when(s_idx == 0)
    def _():
        ka_s[0:CHUNK, :] = jnp.zeros((CHUNK, 2 * PAIR), BF16)
        va_s[0:CHUNK, :] = jnp.zeros((CHUNK, 2 * PAIR), BF16)
        state_s[...] = jnp.zeros(state_s.shape, F32)

    xb = x_ref[...].astype(BF16)
    rcos, rsin = rcos_ref[...], rsin_ref[...]
    h = _dot(xb, win_ref[:, QA0:QA0 + ATTN_WIDTH])
    qa_s[...] = (_rotate(h, rcos, rsin) * QK_SCALE).astype(BF16)
    h = _dot(xb, win_ref[:, KA0:KA0 + 4 * PAIR])
    ka_s[CHUNK:, :] = _rotate(h[:, :2 * PAIR], rcos, rsin).astype(BF16)
    va_s[CHUNK:, :] = h[:, 2 * PAIR:].astype(BF16)

    def retention_projection(j):
        if j == 0:
            qr_s[...] = _rotate(_dot(xb, win_ref[:, QR0:QR0 + RET_WIDTH]), tcos_ref[...], tsin_ref[...])
        elif j == 1:
            kr_s[...] = _rotate(_dot(xb, win_ref[:, KR0:KR0 + RET_WIDTH]), tcos_ref[...], tsin_ref[...]) * QK_SCALE
        elif j == 2:
            vr_s[...] = _dot(xb, win_ref[:, VR0:VR0 + RET_WIDTH]).astype(BF16)
        else:
            gr_s[...] = _dot(xb, win_ref[:, GR0:GR0 + RET_WIDTH])

    lane = lax.broadcasted_iota(jnp.int32, (1, LANES), 1)
    slot_a = (lane % HEAD_DIM) < HALF
    lo = lane < HEAD_DIM
    row = lax.broadcasted_iota(jnp.int32, (LANES, LANES), 0)
    col = lax.broadcasted_iota(jnp.int32, (LANES, LANES), 1)
    state_mask = ((row % HEAD_DIM) < HALF) == (col < HEAD_DIM)
    gn_avg = jnp.where((row < HEAD_DIM) == (col < HEAD_DIM), 1.0 / HEAD_DIM, 0.0).astype(BF16)
    qi = lax.broadcasted_iota(jnp.int32, (CHUNK, 2 * CHUNK), 0)
    kk = lax.broadcasted_iota(jnp.int32, (CHUNK, 2 * CHUNK), 1)
    band = (kk > qi) & (kk <= qi + WINDOW)

    gn_avg2 = jnp.concatenate([gn_avg, gn_avg], axis=0)

    def half_means(v):
        hi = v.astype(BF16)
        lo_part = (v - hi.astype(F32)).astype(BF16)
        return _dot(jnp.concatenate([hi, lo_part], axis=1), gn_avg2)

    for c in range(n_chunks):
        rows = slice(c * CHUNK, (c + 1) * CHUNK)
        kv_rows = slice(c * CHUNK, (c + 2) * CHUNK)
        mask = band & ((kk >= CHUNK) | (s_idx > 0)) if c == 0 else band

        s_att, v_att = [], []
        for g in range(ATTN_KV_HEADS):
            kd = ka_s[kv_rows, g * PAIR:(g + 1) * PAIR]
            vd = va_s[kv_rows, g * PAIR:(g + 1) * PAIR]
            zero = jnp.zeros_like(kd)
            k_cat = jnp.concatenate([jnp.where(slot_a, kd, zero), jnp.where(slot_a, zero, kd)], axis=0)
            v_att.append(jnp.concatenate([jnp.where(lo, vd, zero), jnp.where(lo, zero, vd)], axis=0))
            q_st = jnp.concatenate([qa_s[rows, (2 * g) * PAIR:(2 * g + 1) * PAIR],
                                    qa_s[rows, (2 * g + 1) * PAIR:(2 * g + 2) * PAIR]], axis=0)
            s_att.append(_dot_nt(q_st, k_cat))

        retention_projection(c)

        for g in range(ATTN_KV_HEADS):
            p_rows = []
            for pp in range(2):
                p_cols = []
                for hh in range(2):
                    sink = sinks_ref[layer_ref[0], 2 * (2 * g + pp) + hh]
                    s = s_att[g][pp * CHUNK:(pp + 1) * CHUNK, hh * 2 * CHUNK:(hh + 1) * 2 * CHUNK]
                    s = jnp.where(mask, s, -jnp.inf)
                    m = jnp.maximum(jnp.max(s, axis=-1, keepdims=True), sink)
                    e = jnp.exp(s - m)
                    denom = jnp.sum(e, axis=-1, keepdims=True) + jnp.exp(sink - m)
                    p_cols.append((e / denom).astype(BF16))
                p_rows.append(jnp.concatenate(p_cols, axis=1))
            o = _dot(jnp.concatenate(p_rows, axis=0), v_att[g])
            y_s[rows, (2 * g) * PAIR:(2 * g + 1) * PAIR] = o[:CHUNK].astype(BF16)
            y_s[rows, (2 * g + 1) * PAIR:(2 * g + 2) * PAIR] = o[CHUNK:].astype(BF16)

    ka_s[0:CHUNK, :] = ka_s[ts:ts + CHUNK, :]
    va_s[0:CHUNK, :] = va_s[ts:ts + CHUNK, :]

    for c in range(n_chunks):
        rows = slice(c * CHUNK, (c + 1) * CHUNK)

        s_ret, rhs_ret, qw_ret = [], [], []
        for p in range(N_PAIRS):
            cols = slice(p * PAIR, (p + 1) * PAIR)
            q = qr_s[rows, cols]
            k = kr_s[rows, cols]
            v = vr_s[rows, cols]
            kb = k.astype(BF16)
            zero = jnp.zeros_like(kb)
            k_cat = jnp.concatenate([jnp.where(slot_a, kb, zero), jnp.where(slot_a, zero, kb)], axis=0)
            s_ret.append(_dot_nt(q.astype(BF16), k_cat))
            st = state_s[p]
            rhs_ret.append(jnp.concatenate([jnp.where(lo, v, zero), jnp.where(lo, zero, v),
                                            st.astype(BF16)], axis=0))
            qw_ret.append((q * wq_ref[:, cols]).astype(BF16))
            kv = _dot_tn((k * wk_ref[:, cols]).astype(BF16), v)
            state_s[p] = st * gch_ref[:, cols] + jnp.where(state_mask, kv, 0.0)

        z_s[rows, :] = _dot(y_s[rows, 0:ATTN_WIDTH], wout_ref[0:ATTN_WIDTH, :])

        o_ret = []
        for p in range(N_PAIRS):
            sd = (s_ret[p] * decay_ref[:, (2 * p) * CHUNK:(2 * p + 2) * CHUNK]).astype(BF16)
            o_ret.append(_dot(jnp.concatenate([sd, qw_ret[p]], axis=1), rhs_ret[p]))

        o_all = jnp.concatenate(o_ret, axis=0)
        d = o_all - half_means(o_all)
        on = d * lax.rsqrt(half_means(d * d) + GN_EPS)
        for p in range(N_PAIRS):
            cols = slice(p * PAIR, (p + 1) * PAIR)
            y_s[rows, ATTN_WIDTH + p * PAIR:ATTN_WIDTH + (p + 1) * PAIR] = (
                jax.nn.silu(gr_s[rows, cols]) * on[p * CHUNK:(p + 1) * CHUNK]).astype(BF16)

    half = ts // 2
    for r in range(2):
        rs = slice(r * half, (r + 1) * half)
        y = z_s[rs, :] + _dot(y_s[rs, ATTN_WIDTH:], wout_ref[ATTN_WIDTH:, :])
        o_ref[rs, :] = _layer_norm(ALPHA * x_ref[rs, :] + y, g_ref[...], b_ref[...])


def _mixer(layer, x, sinks, win, wout, g, b, rot_tables, ret_tables, *, batch, seq, ts):
    n_s = seq // ts
    rcos, rsin, tcos, tsin = rot_tables
    decay, wq, wk, gch = ret_tables
    row_spec = pl.BlockSpec((ts, D_MODEL), lambda bi, si, l: (bi * n_s + si, 0))
    tab_spec = pl.BlockSpec((ts, LANES), lambda bi, si, l: (si, 0))
    return pl.pallas_call(
        functools.partial(_mixer_kernel, ts=ts),
        grid_spec=pltpu.PrefetchScalarGridSpec(
            num_scalar_prefetch=1,
            grid=(batch, n_s),
            in_specs=[pl.BlockSpec(memory_space=pltpu.SMEM),
                      row_spec,
                      _layer_resident((D_MODEL, IN_COLS)),
                      _layer_resident((ATTN_WIDTH + RET_WIDTH, D_MODEL)),
                      _layer_resident((1, D_MODEL)),
                      _layer_resident((1, D_MODEL)),
                      tab_spec, tab_spec, tab_spec, tab_spec,
                      _resident((CHUNK, RET_HEADS * CHUNK)),
                      _resident((CHUNK, RET_WIDTH)),
                      _resident((CHUNK, RET_WIDTH)),
                      _resident((1, RET_WIDTH))],
            out_specs=row_spec,
            scratch_shapes=[pltpu.VMEM((ts, ATTN_WIDTH), BF16),
                            pltpu.VMEM((CHUNK + ts, 2 * PAIR), BF16),
                            pltpu.VMEM((CHUNK + ts, 2 * PAIR), BF16),
                            pltpu.VMEM((ts, RET_WIDTH), F32),
                            pltpu.VMEM((ts, RET_WIDTH), F32),
                            pltpu.VMEM((ts, RET_WIDTH), BF16),
                            pltpu.VMEM((ts, RET_WIDTH), F32),
                            pltpu.VMEM((ts, ATTN_WIDTH + RET_WIDTH), BF16),
                            pltpu.VMEM((ts, D_MODEL), F32),
                            pltpu.VMEM((N_PAIRS, LANES, LANES), F32)]),
        out_shape=jax.ShapeDtypeStruct((batch * seq, D_MODEL), F32),
        compiler_params=pltpu.CompilerParams(dimension_semantics=("arbitrary", "arbitrary"),
                                             vmem_limit_bytes=VMEM_LIMIT_BYTES),
        name="mixer",
    )(layer, sinks, x, win, wout, g, b, rcos, rsin, tcos, tsin, decay, wq, wk, gch)


def _pick_tile(n, want):
    t = min(n, want)
    assert n % t == 0 and t % CHUNK == 0, (n, t)
    return t


def kernel(x, w_in, w_out, attn_sinks, ffn1_w_gu, ffn1_w_down, ffn2_w_gu, ffn2_w_down,
           ln1_g, ln1_b, ln2_g, ln2_b, ln3_g, ln3_b):
    batch, seq, d = x.shape
    assert d == D_MODEL and seq % CHUNK == 0
    assert w_in.shape[0] == DEPTH
    t = batch * seq
    tm = _pick_tile(t, 1024)
    ts = _pick_tile(seq, 512)

    rot_tables = _rotation_tables(seq)
    ret_tables = _retention_tables()
    w_in_p = jnp.take(w_in, jnp.asarray(_IN_PERM), axis=2).astype(BF16)
    w_out_b = w_out.astype(BF16)
    wgu1, wd1 = ffn1_w_gu.astype(BF16), ffn1_w_down.astype(BF16)
    wgu2, wd2 = ffn2_w_gu.astype(BF16), ffn2_w_down.astype(BF16)

    def vecs(a):
        return a.reshape(DEPTH, 1, D_MODEL)

    g1, b1, g2, b2, g3, b3 = (vecs(a) for a in (ln1_g, ln1_b, ln2_g, ln2_b, ln3_g, ln3_b))
    h = x.reshape(t, D_MODEL)
    for l in range(DEPTH):
        layer = jnp.full((1,), l, jnp.int32)
        h = _ffn(layer, h, wgu1, wd1, g1, b1, tm=tm, tf=256, sub=256)
        h = _mixer(layer, h, attn_sinks, w_in_p, w_out_b, g2, b2, rot_tables, ret_tables,
                   batch=batch, seq=seq, ts=ts)
        h = _ffn(layer, h, wgu2, wd2, g3, b3, tm=tm, tf=256, sub=256)
    return h.reshape(batch, seq, D_MODEL)
```

```python
import functools

import numpy as np
import jax
import jax.numpy as jnp
from jax import lax
from jax.experimental import pallas as pl
from jax.experimental.pallas import tpu as pltpu

F32 = jnp.float32
BF16 = jnp.bfloat16

D_MODEL = 1024
DEPTH = 4
HEAD_DIM = 64
HALF = HEAD_DIM // 2
ATTN_Q_HEADS = 8
ATTN_KV_HEADS = 2
RET_HEADS = 8
ATTN_WIDTH = ATTN_Q_HEADS * HEAD_DIM
KV_WIDTH = ATTN_KV_HEADS * HEAD_DIM
RET_WIDTH = RET_HEADS * HEAD_DIM
WINDOW = 128
CHUNK = 128
ROPE_THETA = 10000.0
RET_THETA = 10000.0
D_FF = 2816
LN_EPS = 1e-5
GN_EPS = 1e-6
ALPHA = (2 * DEPTH) ** 0.25
QK_SCALE = HEAD_DIM ** -0.5

LANES = 128
PAIR = 2 * HEAD_DIM
N_PAIRS = RET_HEADS // 2

QA0, KA0, VA0 = 0, 512, 768
QR0, KR0, VR0, GR0 = 1024, 1536, 2048, 2560
IN_COLS = 3072

VMEM_LIMIT_BYTES = 56 * 1024 * 1024


def _permute_in_proj(w_in):
    lead = w_in.shape[:2]
    s_ka, s_va, s_qr = ATTN_WIDTH, ATTN_WIDTH + KV_WIDTH, ATTN_WIDTH + 2 * KV_WIDTH
    s_kr, s_vr = s_qr + RET_WIDTH, s_qr + 2 * RET_WIDTH

    def half_split(w, pairs):
        w = w.reshape(lead + (pairs, 2, 2, HALF))
        return jnp.swapaxes(w, -3, -2).reshape(lead + (pairs * PAIR,))

    def interleaved(w, pairs):
        w = w.reshape(lead + (pairs, 2, HALF, 2))
        return jnp.transpose(w, (0, 1, 2, 5, 3, 4)).reshape(lead + (pairs * PAIR,))

    qa = half_split(w_in[..., :s_ka], ATTN_Q_HEADS // 2)
    ka = w_in[..., s_ka:s_va].reshape(lead + (ATTN_KV_HEADS, 2, 1, HALF))
    ka = jnp.broadcast_to(ka, lead + (ATTN_KV_HEADS, 2, 2, HALF)).reshape(lead + (2 * PAIR,))
    va = w_in[..., s_va:s_qr].reshape(lead + (ATTN_KV_HEADS, 1, HEAD_DIM))
    va = jnp.broadcast_to(va, lead + (ATTN_KV_HEADS, 2, HEAD_DIM)).reshape(lead + (2 * PAIR,))
    qr = interleaved(w_in[..., s_qr:s_kr], N_PAIRS)
    kr = interleaved(w_in[..., s_kr:s_vr], N_PAIRS)
    out = jnp.concatenate([qa, ka, va, qr, kr, w_in[..., s_vr:]], axis=-1).astype(BF16)
    assert out.shape == lead + (IN_COLS,)
    return out


def _rotation_tables(seq):
    pos = jnp.arange(seq, dtype=F32)
    inv_freq = ROPE_THETA ** (-jnp.arange(0, HEAD_DIM, 2, dtype=F32) / HEAD_DIM)
    ang = pos[:, None] * inv_freq[None, :]
    ret_freq = 1.0 / (RET_THETA ** jnp.linspace(0.0, 1.0, HEAD_DIM // 2, dtype=F32))
    rang = pos[:, None] * ret_freq[None, :]

    def lay(c, s):
        return [c, c, c, c, -s, -s, s, s]

    return jnp.concatenate(lay(jnp.cos(ang), jnp.sin(ang)) + lay(jnp.cos(rang), jnp.sin(rang)), axis=1)


def _retention_tables():
    h = jnp.arange(RET_HEADS, dtype=F32)
    log_gamma = jnp.log1p(-jnp.exp2(-5.0 - h))
    idx = jnp.arange(CHUNK, dtype=F32)
    diff = idx[:, None] - idx[None, :]
    decay = jnp.where(diff[None] >= 0,
                      jnp.exp(jnp.maximum(diff, 0.0)[None] * log_gamma[:, None, None]), 0.0)
    decay = jnp.transpose(decay, (1, 0, 2)).reshape(CHUNK, RET_HEADS * CHUNK)
    w_k = jnp.exp((CHUNK - 1.0 - idx)[None, :] * log_gamma[:, None])
    w_q = jnp.exp((idx + 1.0)[None, :] * log_gamma[:, None])
    g_chunk = jnp.exp(CHUNK * log_gamma)

    def rolled(w):
        w = w.T.reshape(CHUNK, N_PAIRS, 2, 1)
        return jnp.broadcast_to(jnp.concatenate([w, w], axis=2), (CHUNK, N_PAIRS, 4, HALF)).reshape(CHUNK, RET_WIDTH)

    g_nat = jnp.repeat(g_chunk, HEAD_DIM)[None, :]
    return decay, rolled(w_q), rolled(w_k), g_nat


def _layer_norm(z, g, b):
    mu = jnp.mean(z, axis=-1, keepdims=True)
    d = z - mu
    var = jnp.mean(d * d, axis=-1, keepdims=True)
    return d * lax.rsqrt(var + LN_EPS) * g + b


def _dot(a, b):
    return jnp.dot(a, b, preferred_element_type=F32)


def _dot_nt(a, b):
    return lax.dot_general(a, b, (((1,), (1,)), ((), ())), preferred_element_type=F32)


def _dot_tn(a, b):
    return lax.dot_general(a, b, (((0,), (0,)), ((), ())), preferred_element_type=F32)


def _ffn_kernel(layer_ref, x_ref, wgu_ref, wd_ref, g_ref, b_ref, o_ref, h_scr, *, tf, sub):
    for r in range(x_ref.shape[0] // sub):
        rs = slice(r * sub, (r + 1) * sub)
        x = x_ref[rs, :]
        xb = x.astype(BF16)
        for c in range(D_FF // tf):
            a = _dot(xb, wgu_ref[:, c * tf:(c + 1) * tf])
            u = _dot(xb, wgu_ref[:, D_FF + c * tf:D_FF + (c + 1) * tf])
            h_scr[rs, c * tf:(c + 1) * tf] = (jax.nn.silu(a) * u).astype(BF16)
        y = _dot(h_scr[rs, :], wd_ref[...])
        o_ref[rs, :] = _layer_norm(ALPHA * x + 0.5 * y, g_ref[...], b_ref[...])


def _resident(shape):
    return pl.BlockSpec(shape, lambda *_: (0,) * len(shape), pipeline_mode=pl.Buffered(1))


def _layer_resident(shape):
    return pl.BlockSpec((None,) + shape, lambda *a: (a[-1][0],) + (0,) * len(shape),
                        pipeline_mode=pl.Buffered(1))


def _ffn(layer, x, wgu, wd, g, b, *, tm, tf, sub):
    t = x.shape[0]
    return pl.pallas_call(
        functools.partial(_ffn_kernel, tf=tf, sub=sub),
        grid_spec=pltpu.PrefetchScalarGridSpec(
            num_scalar_prefetch=1,
            grid=(t // tm,),
            in_specs=[pl.BlockSpec((tm, D_MODEL), lambda i, l: (i, 0)),
                      _layer_resident((D_MODEL, 2 * D_FF)),
                      _layer_resident((D_FF, D_MODEL)),
                      _layer_resident((1, D_MODEL)),
                      _layer_resident((1, D_MODEL))],
            out_specs=pl.BlockSpec((tm, D_MODEL), lambda i, l: (i, 0)),
            scratch_shapes=[pltpu.VMEM((tm, D_FF), BF16)]),
        out_shape=jax.ShapeDtypeStruct((t, D_MODEL), F32),
        compiler_params=pltpu.CompilerParams(dimension_semantics=("arbitrary",),
                                             vmem_limit_bytes=VMEM_LIMIT_BYTES),
        name="ffn",
    )(layer, x, wgu, wd, g, b)


def _rotate(h, cos, sin):
    outs = []
    for j in range(h.shape[1] // LANES):
        hj = h[:, j * LANES:(j + 1) * LANES]
        outs.append(hj * cos + pltpu.roll(hj, PAIR // 2, 1) * sin)
    return jnp.concatenate(outs, axis=1)


def _mixer_kernel(layer_ref, sinks_ref, x_ref, win_ref, wout_ref, g_ref, b_ref,
                  rot_ref,
                  decay_ref, wq_ref, wk_ref, gch_ref,
                  o_ref,
                  qa_s, ka_s, va_s, qr_s, kr_s, vr_s, gr_s, y_s, z_s, state_s, *, ts):
    s_idx = pl.program_id(1)
    n_chunks = ts // CHUNK
    assert n_chunks >= 4

    @pl.when(s_idx == 0)
    def _():
        ka_s[0:CHUNK, :] = jnp.zeros((CHUNK, 2 * PAIR), BF16)
        va_s[0:CHUNK, :] = jnp.zeros((CHUNK, 2 * PAIR), BF16)
        state_s[...] = jnp.zeros(state_s.shape, F32)

    xb = x_ref[...].astype(BF16)
    rcos, rsin = rot_ref[:, 0:LANES], rot_ref[:, LANES:2 * LANES]
    h = _dot(xb, win_ref[:, QA0:QA0 + ATTN_WIDTH])
    qa_s[...] = (_rotate(h, rcos, rsin) * QK_SCALE).astype(BF16)
    h = _dot(xb, win_ref[:, KA0:KA0 + 4 * PAIR])
    ka_s[CHUNK:, :] = _rotate(h[:, :2 * PAIR], rcos, rsin).astype(BF16)
    va_s[CHUNK:, :] = h[:, 2 * PAIR:].astype(BF16)

    def tcos():
        return rot_ref[:, 2 * LANES:3 * LANES]

    def tsin():
        return rot_ref[:, 3 * LANES:4 * LANES]

    def retention_projection(j):
        if j == 0:
            qr_s[...] = _rotate(_dot(xb, win_ref[:, QR0:QR0 + RET_WIDTH]), tcos(), tsin())
        elif j == 1:
            kr_s[...] = _rotate(_dot(xb, win_ref[:, KR0:KR0 + RET_WIDTH]), tcos(), tsin()) * QK_SCALE
        elif j == 2:
            vr_s[...] = _dot(xb, win_ref[:, VR0:VR0 + RET_WIDTH]).astype(BF16)
        else:
            gr_s[...] = _dot(xb, win_ref[:, GR0:GR0 + RET_WIDTH])

    lane = lax.broadcasted_iota(jnp.int32, (1, LANES), 1)
    slot_a = (lane % HEAD_DIM) < HALF
    lo = lane < HEAD_DIM
    row = lax.broadcasted_iota(jnp.int32, (LANES, LANES), 0)
    col = lax.broadcasted_iota(jnp.int32, (LANES, LANES), 1)
    state_mask = ((row % HEAD_DIM) < HALF) == (col < HEAD_DIM)
    gn_avg = jnp.where((row < HEAD_DIM) == (col < HEAD_DIM), 1.0 / HEAD_DIM, 0.0).astype(BF16)
    qi = lax.broadcasted_iota(jnp.int32, (CHUNK, 2 * CHUNK), 0)
    kk = lax.broadcasted_iota(jnp.int32, (CHUNK, 2 * CHUNK), 1)
    band = (kk > qi) & (kk <= qi + WINDOW)

    gn_avg2 = jnp.concatenate([gn_avg, gn_avg], axis=0)

    def half_means(v):
        hi = v.astype(BF16)
        lo_part = (v - hi.astype(F32)).astype(BF16)
        return _dot(jnp.concatenate([hi, lo_part], axis=1), gn_avg2)

    for c in range(n_chunks):
        rows = slice(c * CHUNK, (c + 1) * CHUNK)
        kv_rows = slice(c * CHUNK, (c + 2) * CHUNK)
        mask = band & ((kk >= CHUNK) | (s_idx > 0)) if c == 0 else band

        s_att, v_att = [], []
        for g in range(ATTN_KV_HEADS):
            kd = ka_s[kv_rows, g * PAIR:(g + 1) * PAIR]
            vd = va_s[kv_rows, g * PAIR:(g + 1) * PAIR]
            zero = jnp.zeros_like(kd)
            k_cat = jnp.concatenate([jnp.where(slot_a, kd, zero), jnp.where(slot_a, zero, kd)], axis=0)
            v_att.append(jnp.concatenate([jnp.where(lo, vd, zero), jnp.where(lo, zero, vd)], axis=0))
            q_st = jnp.concatenate([qa_s[rows, (2 * g) * PAIR:(2 * g + 1) * PAIR],
                                    qa_s[rows, (2 * g + 1) * PAIR:(2 * g + 2) * PAIR]], axis=0)
            s_att.append(_dot_nt(q_st, k_cat))

        if c < 4:
            retention_projection(c)

        for g in range(ATTN_KV_HEADS):
            p_rows = []
            for pp in range(2):
                p_cols = []
                for hh in range(2):
                    sink = sinks_ref[layer_ref[0], 2 * (2 * g + pp) + hh]
                    s = s_att[g][pp * CHUNK:(pp + 1) * CHUNK, hh * 2 * CHUNK:(hh + 1) * 2 * CHUNK]
                    s = jnp.where(mask, s, -jnp.inf)
                    m = jnp.maximum(jnp.max(s, axis=-1, keepdims=True), sink)
                    e = jnp.exp(s - m)
                    denom = jnp.sum(e, axis=-1, keepdims=True) + jnp.exp(sink - m)
                    p_cols.append((e / denom).astype(BF16))
                p_rows.append(jnp.concatenate(p_cols, axis=1))
            o = _dot(jnp.concatenate(p_rows, axis=0), v_att[g])
            y_s[rows, (2 * g) * PAIR:(2 * g + 1) * PAIR] = o[:CHUNK].astype(BF16)
            y_s[rows, (2 * g + 1) * PAIR:(2 * g + 2) * PAIR] = o[CHUNK:].astype(BF16)

    ka_s[0:CHUNK, :] = ka_s[ts:ts + CHUNK, :]
    va_s[0:CHUNK, :] = va_s[ts:ts + CHUNK, :]

    for c in range(n_chunks):
        rows = slice(c * CHUNK, (c + 1) * CHUNK)

        s_ret, rhs_ret, qw_ret = [], [], []
        for p in range(N_PAIRS):
            cols = slice(p * PAIR, (p + 1) * PAIR)
            q = qr_s[rows, cols]
            k = kr_s[rows, cols]
            v = vr_s[rows, cols]
            kb = k.astype(BF16)
            zero = jnp.zeros_like(kb)
            k_cat = jnp.concatenate([jnp.where(slot_a, kb, zero), jnp.where(slot_a, zero, kb)], axis=0)
            s_ret.append(_dot_nt(q.astype(BF16), k_cat))
            st = state_s[p]
            rhs_ret.append(jnp.concatenate([jnp.where(lo, v, zero), jnp.where(lo, zero, v),
                                            st.astype(BF16)], axis=0))
            qw_ret.append((q * wq_ref[:, cols]).astype(BF16))
            kv = _dot_tn((k * wk_ref[:, cols]).astype(BF16), v)
            state_s[p] = st * gch_ref[:, cols] + jnp.where(state_mask, kv, 0.0)

        z_s[rows, :] = _dot(y_s[rows, 0:ATTN_WIDTH], wout_ref[0:ATTN_WIDTH, :])

        o_ret = []
        for p in range(N_PAIRS):
            sd = (s_ret[p] * decay_ref[:, (2 * p) * CHUNK:(2 * p + 2) * CHUNK]).astype(BF16)
            o_ret.append(_dot(jnp.concatenate([sd, qw_ret[p]], axis=1), rhs_ret[p]))

        o_all = jnp.concatenate(o_ret, axis=0)
        d = o_all - half_means(o_all)
        on = d * lax.rsqrt(half_means(d * d) + GN_EPS)
        for p in range(N_PAIRS):
            cols = slice(p * PAIR, (p + 1) * PAIR)
            y_s[rows, ATTN_WIDTH + p * PAIR:ATTN_WIDTH + (p + 1) * PAIR] = (
                jax.nn.silu(gr_s[rows, cols]) * on[p * CHUNK:(p + 1) * CHUNK]).astype(BF16)

    for r in range(ts // 256):
        rs = slice(r * 256, (r + 1) * 256)
        y = z_s[rs, :] + _dot(y_s[rs, ATTN_WIDTH:], wout_ref[ATTN_WIDTH:, :])
        o_ref[rs, :] = _layer_norm(ALPHA * x_ref[rs, :] + y, g_ref[...], b_ref[...])


def _mixer(layer, x, sinks, win, wout, g, b, rot_tables, ret_tables, *, batch, seq, ts):
    n_s = seq // ts
    decay, wq, wk, gch = ret_tables
    row_spec = pl.BlockSpec((ts, D_MODEL), lambda bi, si, l: (bi * n_s + si, 0))
    tab_spec = pl.BlockSpec((ts, 4 * LANES), lambda bi, si, l: (si, 0))
    return pl.pallas_call(
        functools.partial(_mixer_kernel, ts=ts),
        grid_spec=pltpu.PrefetchScalarGridSpec(
            num_scalar_prefetch=1,
            grid=(batch, n_s),
            in_specs=[pl.BlockSpec(memory_space=pltpu.SMEM),
                      row_spec,
                      _layer_resident((D_MODEL, IN_COLS)),
                      _layer_resident((ATTN_WIDTH + RET_WIDTH, D_MODEL)),
                      _layer_resident((1, D_MODEL)),
                      _layer_resident((1, D_MODEL)),
                      tab_spec,
                      _resident((CHUNK, RET_HEADS * CHUNK)),
                      _resident((CHUNK, RET_WIDTH)),
                      _resident((CHUNK, RET_WIDTH)),
                      _resident((1, RET_WIDTH))],
            out_specs=row_spec,
            scratch_shapes=[pltpu.VMEM((ts, ATTN_WIDTH), BF16),
                            pltpu.VMEM((CHUNK + ts, 2 * PAIR), BF16),
                            pltpu.VMEM((CHUNK + ts, 2 * PAIR), BF16),
                            pltpu.VMEM((ts, RET_WIDTH), F32),
                            pltpu.VMEM((ts, RET_WIDTH), F32),
                            pltpu.VMEM((ts, RET_WIDTH), BF16),
                            pltpu.VMEM((ts, RET_WIDTH), F32),
                            pltpu.VMEM((ts, ATTN_WIDTH + RET_WIDTH), BF16),
                            pltpu.VMEM((ts, D_MODEL), F32),
                            pltpu.VMEM((N_PAIRS, LANES, LANES), F32)]),
        out_shape=jax.ShapeDtypeStruct((batch * seq, D_MODEL), F32),
        compiler_params=pltpu.CompilerParams(dimension_semantics=("arbitrary", "arbitrary"),
                                             vmem_limit_bytes=VMEM_LIMIT_BYTES),
        name="mixer",
    )(layer, sinks, x, win, wout, g, b, rot_tables, decay, wq, wk, gch)


def _pick_tile(n, want):
    t = min(n, want)
    assert n % t == 0 and t % CHUNK == 0, (n, t)
    return t


def kernel(x, w_in, w_out, attn_sinks, ffn1_w_gu, ffn1_w_down, ffn2_w_gu, ffn2_w_down,
           ln1_g, ln1_b, ln2_g, ln2_b, ln3_g, ln3_b):
    batch, seq, d = x.shape
    assert d == D_MODEL and seq % CHUNK == 0
    assert w_in.shape[0] == DEPTH
    t = batch * seq
    tm = _pick_tile(t, 1024)
    ts = _pick_tile(seq, 1024)

    rot_tables = _rotation_tables(seq)
    ret_tables = _retention_tables()
    w_in_p = _permute_in_proj(w_in)
    w_out_b = w_out.astype(BF16)
    wgu1, wd1 = ffn1_w_gu.astype(BF16), ffn1_w_down.astype(BF16)
    wgu2, wd2 = ffn2_w_gu.astype(BF16), ffn2_w_down.astype(BF16)

    def vecs(a):
        return a.reshape(DEPTH, 1, D_MODEL)

    g1, b1, g2, b2, g3, b3 = (vecs(a) for a in (ln1_g, ln1_b, ln2_g, ln2_b, ln3_g, ln3_b))
    h = x.reshape(t, D_MODEL)
    for l in range(DEPTH):
        layer = jnp.full((1,), l, jnp.int32)
        h = _ffn(layer, h, wgu1, wd1, g1, b1, tm=tm, tf=256, sub=256)
        h = _mixer(layer, h, attn_sinks, w_in_p, w_out_b, g2, b2, rot_tables, ret_tables,
                   batch=batch, seq=seq, ts=ts)
        h = _ffn(layer, h, wgu2, wd2, g3, b3, tm=tm, tf=256, sub=256)
    return h.reshape(batch, seq, D_MODEL)
```

```python
import functools

import numpy as np
import jax
import jax.numpy as jnp
from jax import lax
from jax.experimental import pallas as pl
from jax.experimental.pallas import tpu as pltpu

F32 = jnp.float32
BF16 = jnp.bfloat16

D_MODEL = 1024
DEPTH = 4
HEAD_DIM = 64
HALF = HEAD_DIM // 2
ATTN_Q_HEADS = 8
ATTN_KV_HEADS = 2
RET_HEADS = 8
ATTN_WIDTH = ATTN_Q_HEADS * HEAD_DIM
KV_WIDTH = ATTN_KV_HEADS * HEAD_DIM
RET_WIDTH = RET_HEADS * HEAD_DIM
WINDOW = 128
CHUNK = 128
ROPE_THETA = 10000.0
RET_THETA = 10000.0
D_FF = 2816
LN_EPS = 1e-5
GN_EPS = 1e-6
ALPHA = (2 * DEPTH) ** 0.25
QK_SCALE = HEAD_DIM ** -0.5

LANES = 128
PAIR = 2 * HEAD_DIM
N_PAIRS = RET_HEADS // 2

QA0, KA0, VA0 = 0, 512, 768
QR0, KR0, VR0, GR0 = 1024, 1536, 2048, 2560
IN_COLS = 3072

VMEM_LIMIT_BYTES = 56 * 1024 * 1024


def _permute_in_proj(w_in):
    lead = w_in.shape[:2]
    s_ka, s_va, s_qr = ATTN_WIDTH, ATTN_WIDTH + KV_WIDTH, ATTN_WIDTH + 2 * KV_WIDTH
    s_kr, s_vr = s_qr + RET_WIDTH, s_qr + 2 * RET_WIDTH

    def half_split(w, pairs):
        w = w.reshape(lead + (pairs, 2, 2, HALF))
        return jnp.swapaxes(w, -3, -2).reshape(lead + (pairs * PAIR,))

    def interleaved(w, pairs):
        w = w.reshape(lead + (pairs, 2, HALF, 2))
        return jnp.transpose(w, (0, 1, 2, 5, 3, 4)).reshape(lead + (pairs * PAIR,))

    qa = half_split(w_in[..., :s_ka], ATTN_Q_HEADS // 2)
    ka = w_in[..., s_ka:s_va].reshape(lead + (ATTN_KV_HEADS, 2, 1, HALF))
    ka = jnp.broadcast_to(ka, lead + (ATTN_KV_HEADS, 2, 2, HALF)).reshape(lead + (2 * PAIR,))
    va = w_in[..., s_va:s_qr].reshape(lead + (ATTN_KV_HEADS, 1, HEAD_DIM))
    va = jnp.broadcast_to(va, lead + (ATTN_KV_HEADS, 2, HEAD_DIM)).reshape(lead + (2 * PAIR,))
    qr = interleaved(w_in[..., s_qr:s_kr], N_PAIRS)
    kr = interleaved(w_in[..., s_kr:s_vr], N_PAIRS)
    out = jnp.concatenate([qa, ka, va, qr, kr, w_in[..., s_vr:]], axis=-1).astype(BF16)
    assert out.shape == lead + (IN_COLS,)
    return out


def _rotation_tables(seq):
    pos = jnp.arange(seq, dtype=F32)
    inv_freq = ROPE_THETA ** (-jnp.arange(0, HEAD_DIM, 2, dtype=F32) / HEAD_DIM)
    ang = pos[:, None] * inv_freq[None, :]
    ret_freq = 1.0 / (RET_THETA ** jnp.linspace(0.0, 1.0, HEAD_DIM // 2, dtype=F32))
    rang = pos[:, None] * ret_freq[None, :]

    def lay(c, s):
        return [c, c, c, c, -s, -s, s, s]

    return jnp.concatenate(lay(jnp.cos(ang), jnp.sin(ang)) + lay(jnp.cos(rang), jnp.sin(rang)), axis=1)


def _retention_tables():
    h = jnp.arange(RET_HEADS, dtype=F32)
    log_gamma = jnp.log1p(-jnp.exp2(-5.0 - h))
    idx = jnp.arange(CHUNK, dtype=F32)
    diff = idx[:, None] - idx[None, :]
    decay = jnp.where(diff[None] >= 0,
                      jnp.exp(jnp.maximum(diff, 0.0)[None] * log_gamma[:, None, None]), 0.0)
    decay = jnp.transpose(decay, (1, 0, 2)).reshape(CHUNK, RET_HEADS * CHUNK)
    w_k = jnp.exp((CHUNK - 1.0 - idx)[None, :] * log_gamma[:, None])
    w_q = jnp.exp((idx + 1.0)[None, :] * log_gamma[:, None])
    g_chunk = jnp.exp(CHUNK * log_gamma)

    def rolled(w):
        w = w.T.reshape(CHUNK, N_PAIRS, 2, 1)
        return jnp.broadcast_to(jnp.concatenate([w, w], axis=2), (CHUNK, N_PAIRS, 4, HALF)).reshape(CHUNK, RET_WIDTH)

    g_nat = jnp.repeat(g_chunk, HEAD_DIM)[None, :]
    return decay, rolled(w_q), rolled(w_k), g_nat


def _layer_norm(z, g, b):
    mu = jnp.mean(z, axis=-1, keepdims=True)
    d = z - mu
    var = jnp.mean(d * d, axis=-1, keepdims=True)
    return d * lax.rsqrt(var + LN_EPS) * g + b


def _dot(a, b):
    return jnp.dot(a, b, preferred_element_type=F32)


def _dot_nt(a, b):
    return lax.dot_general(a, b, (((1,), (1,)), ((), ())), preferred_element_type=F32)


def _dot_tn(a, b):
    return lax.dot_general(a, b, (((0,), (0,)), ((), ())), preferred_element_type=F32)


def _ffn_kernel(layer_ref, x_ref, wgu_ref, wd_ref, g_ref, b_ref, o_ref, h_scr, *, tf, sub):
    for r in range(x_ref.shape[0] // sub):
        rs = slice(r * sub, (r + 1) * sub)
        x = x_ref[rs, :]
        xb = x.astype(BF16)
        for c in range(D_FF // tf):
            a = _dot(xb, wgu_ref[:, c * tf:(c + 1) * tf])
            u = _dot(xb, wgu_ref[:, D_FF + c * tf:D_FF + (c + 1) * tf])
            h_scr[rs, c * tf:(c + 1) * tf] = (jax.nn.silu(a) * u).astype(BF16)
        y = _dot(h_scr[rs, :], wd_ref[...])
        o_ref[rs, :] = _layer_norm(ALPHA * x + 0.5 * y, g_ref[...], b_ref[...])


def _resident(shape):
    return pl.BlockSpec(shape, lambda *_: (0,) * len(shape), pipeline_mode=pl.Buffered(1))


def _layer_resident(shape):
    return pl.BlockSpec((None,) + shape, lambda *a: (a[-1][0],) + (0,) * len(shape),
                        pipeline_mode=pl.Buffered(1))


def _ffn(layer, x, wgu, wd, g, b, *, tm, tf, sub):
    t = x.shape[0]
    return pl.pallas_call(
        functools.partial(_ffn_kernel, tf=tf, sub=sub),
        grid_spec=pltpu.PrefetchScalarGridSpec(
            num_scalar_prefetch=1,
            grid=(t // tm,),
            in_specs=[pl.BlockSpec((tm, D_MODEL), lambda i, l: (i, 0)),
                      _layer_resident((D_MODEL, 2 * D_FF)),
                      _layer_resident((D_FF, D_MODEL)),
                      _layer_resident((1, D_MODEL)),
                      _layer_resident((1, D_MODEL))],
            out_specs=pl.BlockSpec((tm, D_MODEL), lambda i, l: (i, 0)),
            scratch_shapes=[pltpu.VMEM((tm, D_FF), BF16)]),
        out_shape=jax.ShapeDtypeStruct((t, D_MODEL), F32),
        compiler_params=pltpu.CompilerParams(dimension_semantics=("arbitrary",),
                                             vmem_limit_bytes=VMEM_LIMIT_BYTES),
        name="ffn",
    )(layer, x, wgu, wd, g, b)


def _rotate(h, cos, sin):
    outs = []
    for j in range(h.shape[1] // LANES):
        hj = h[:, j * LANES:(j + 1) * LANES]
        outs.append(hj * cos + pltpu.roll(hj, PAIR // 2, 1) * sin)
    return jnp.concatenate(outs, axis=1)


def _mixer_kernel(layer_ref, sinks_ref, x_ref, win_ref, wout_ref, g_ref, b_ref,
                  rot_ref,
                  decay_ref, wq_ref, wk_ref, gch_ref,
                  o_ref,
                  qa_s, ka_s, va_s, qr_s, kr_s, vr_s, gr_s, y_s, z_s, state_s, *, ts):
    s_idx = pl.program_id(1)
    n_chunks = ts // CHUNK
    assert n_chunks >= 4

    @pl.when(s_idx == 0)
    def _():
        ka_s[0:CHUNK, :] = jnp.zeros((CHUNK, 2 * PAIR), BF16)
        va_s[0:CHUNK, :] = jnp.zeros((CHUNK, 2 * PAIR), BF16)
        state_s[...] = jnp.zeros(state_s.shape, F32)

    xb = x_ref[...].astype(BF16)
    rcos, rsin = rot_ref[:, 0:LANES], rot_ref[:, LANES:2 * LANES]
    h = _dot(xb, win_ref[:, QA0:QA0 + ATTN_WIDTH])
    qa_s[...] = (_rotate(h, rcos, rsin) * QK_SCALE).astype(BF16)
    h = _dot(xb, win_ref[:, KA0:KA0 + 4 * PAIR])
    ka_s[CHUNK:, :] = _rotate(h[:, :2 * PAIR], rcos, rsin).astype(BF16)
    va_s[CHUNK:, :] = h[:, 2 * PAIR:].astype(BF16)

    def tcos():
        return rot_ref[:, 2 * LANES:3 * LANES]

    def tsin():
        return rot_ref[:, 3 * LANES:4 * LANES]

    def retention_projection(j):
        if j == 0:
            qr_s[...] = _rotate(_dot(xb, win_ref[:, QR0:QR0 + RET_WIDTH]), tcos(), tsin())
        elif j == 1:
            kr_s[...] = _rotate(_dot(xb, win_ref[:, KR0:KR0 + RET_WIDTH]), tcos(), tsin()) * QK_SCALE
        elif j == 2:
            vr_s[...] = _dot(xb, win_ref[:, VR0:VR0 + RET_WIDTH]).astype(BF16)
        else:
            gr_s[...] = _dot(xb, win_ref[:, GR0:GR0 + RET_WIDTH])

    lane = lax.broadcasted_iota(jnp.int32, (1, LANES), 1)
    slot_a = (lane % HEAD_DIM) < HALF
    lo = lane < HEAD_DIM
    row = lax.broadcasted_iota(jnp.int32, (LANES, LANES), 0)
    col = lax.broadcasted_iota(jnp.int32, (LANES, LANES), 1)
    state_mask = ((row % HEAD_DIM) < HALF) == (col < HEAD_DIM)
    gn_avg = jnp.where((row < HEAD_DIM) == (col < HEAD_DIM), 1.0 / HEAD_DIM, 0.0).astype(BF16)
    qi = lax.broadcasted_iota(jnp.int32, (CHUNK, 2 * CHUNK), 0)
    kk = lax.broadcasted_iota(jnp.int32, (CHUNK, 2 * CHUNK), 1)
    band = (kk > qi) & (kk <= qi + WINDOW)

    gn_avg2 = jnp.concatenate([gn_avg, gn_avg], axis=0)

    def half_means(v):
        hi = v.astype(BF16)
        lo_part = (v - hi.astype(F32)).astype(BF16)
        return _dot(jnp.concatenate([hi, lo_part], axis=1), gn_avg2)

    for c in range(n_chunks):
        rows = slice(c * CHUNK, (c + 1) * CHUNK)
        kv_rows = slice(c * CHUNK, (c + 2) * CHUNK)
        mask = band & ((kk >= CHUNK) | (s_idx > 0)) if c == 0 else band

        s_att, v_att = [], []
        for g in range(ATTN_KV_HEADS):
            kd = ka_s[kv_rows, g * PAIR:(g + 1) * PAIR]
            vd = va_s[kv_rows, g * PAIR:(g + 1) * PAIR]
            zero = jnp.zeros_like(kd)
            k_cat = jnp.concatenate([jnp.where(slot_a, kd, zero), jnp.where(slot_a, zero, kd)], axis=0)
            v_att.append(jnp.concatenate([jnp.where(lo, vd, zero), jnp.where(lo, zero, vd)], axis=0))
            q_st = jnp.concatenate([qa_s[rows, (2 * g) * PAIR:(2 * g + 1) * PAIR],
                                    qa_s[rows, (2 * g + 1) * PAIR:(2 * g + 2) * PAIR]], axis=0)
            s_att.append(_dot_nt(q_st, k_cat))

        if c < 4:
            retention_projection(c)

        for g in range(ATTN_KV_HEADS):
            p_rows = []
            for pp in range(2):
                p_cols = []
                for hh in range(2):
                    sink = sinks_ref[layer_ref[0], 2 * (2 * g + pp) + hh]
                    s = s_att[g][pp * CHUNK:(pp + 1) * CHUNK, hh * 2 * CHUNK:(hh + 1) * 2 * CHUNK]
                    s = jnp.where(mask, s, -jnp.inf)
                    m = jnp.maximum(jnp.max(s, axis=-1, keepdims=True), sink)
                    e = jnp.exp(s - m)
                    denom = jnp.sum(e, axis=-1, keepdims=True) + jnp.exp(sink - m)
                    p_cols.append((e / denom).astype(BF16))
                p_rows.append(jnp.concatenate(p_cols, axis=1))
            o = _dot(jnp.concatenate(p_rows, axis=0), v_att[g])
            y_s[rows, (2 * g) * PAIR:(2 * g + 1) * PAIR] = o[:CHUNK].astype(BF16)
            y_s[rows, (2 * g + 1) * PAIR:(2 * g + 2) * PAIR] = o[CHUNK:].astype(BF16)

    ka_s[0:CHUNK, :] = ka_s[ts:ts + CHUNK, :]
    va_s[0:CHUNK, :] = va_s[ts:ts + CHUNK, :]

    staged = [dict() for _ in range(n_chunks)]

    def ret_scores(c):
        rows = slice(c * CHUNK, (c + 1) * CHUNK)
        s_ret, rhs_ret, qw_ret = [], [], []
        for p in range(N_PAIRS):
            cols = slice(p * PAIR, (p + 1) * PAIR)
            q = qr_s[rows, cols]
            k = kr_s[rows, cols]
            v = vr_s[rows, cols]
            kb = k.astype(BF16)
            zero = jnp.zeros_like(kb)
            k_cat = jnp.concatenate([jnp.where(slot_a, kb, zero), jnp.where(slot_a, zero, kb)], axis=0)
            s_ret.append(_dot_nt(q.astype(BF16), k_cat))
            st = state_s[p]
            rhs_ret.append(jnp.concatenate([jnp.where(lo, v, zero), jnp.where(lo, zero, v),
                                            st.astype(BF16)], axis=0))
            qw_ret.append((q * wq_ref[:, cols]).astype(BF16))
            kv = _dot_tn((k * wk_ref[:, cols]).astype(BF16), v)
            state_s[p] = st * gch_ref[:, cols] + jnp.where(state_mask, kv, 0.0)
        z_s[rows, :] = _dot(y_s[rows, 0:ATTN_WIDTH], wout_ref[0:ATTN_WIDTH, :])
        staged[c].update(s=s_ret, rhs=rhs_ret, qw=qw_ret)

    def ret_outputs(c):
        st = staged[c]
        o_ret = []
        for p in range(N_PAIRS):
            sd = (st["s"][p] * decay_ref[:, (2 * p) * CHUNK:(2 * p + 2) * CHUNK]).astype(BF16)
            o_ret.append(_dot(jnp.concatenate([sd, st["qw"][p]], axis=1), st["rhs"][p]))
        st["o"] = jnp.concatenate(o_ret, axis=0)

    def gn_center(c):
        staged[c]["d"] = staged[c]["o"] - half_means(staged[c]["o"])

    def gn_finish(c):
        rows = slice(c * CHUNK, (c + 1) * CHUNK)
        d = staged[c]["d"]
        on = d * lax.rsqrt(half_means(d * d) + GN_EPS)
        for p in range(N_PAIRS):
            cols = slice(p * PAIR, (p + 1) * PAIR)
            y_s[rows, ATTN_WIDTH + p * PAIR:ATTN_WIDTH + (p + 1) * PAIR] = (
                jax.nn.silu(gr_s[rows, cols]) * on[p * CHUNK:(p + 1) * CHUNK]).astype(BF16)

    ret_scores(0)
    ret_outputs(0)
    for c in range(1, n_chunks):
        ret_scores(c)
        gn_center(c - 1)
        ret_outputs(c)
        gn_finish(c - 1)
    gn_center(n_chunks - 1)
    gn_finish(n_chunks - 1)

    for r in range(ts // 256):
        rs = slice(r * 256, (r + 1) * 256)
        y = z_s[rs, :] + _dot(y_s[rs, ATTN_WIDTH:], wout_ref[ATTN_WIDTH:, :])
        o_ref[rs, :] = _layer_norm(ALPHA * x_ref[rs, :] + y, g_ref[...], b_ref[...])


def _mixer(layer, x, sinks, win, wout, g, b, rot_tables, ret_tables, *, batch, seq, ts):
    n_s = seq // ts
    decay, wq, wk, gch = ret_tables
    row_spec = pl.BlockSpec((ts, D_MODEL), lambda bi, si, l: (bi * n_s + si, 0))
    tab_spec = pl.BlockSpec((ts, 4 * LANES), lambda bi, si, l: (si, 0))
    return pl.pallas_call(
        functools.partial(_mixer_kernel, ts=ts),
        grid_spec=pltpu.PrefetchScalarGridSpec(
            num_scalar_prefetch=1,
            grid=(batch, n_s),
            in_specs=[pl.BlockSpec(memory_space=pltpu.SMEM),
                      row_spec,
                      _layer_resident((D_MODEL, IN_COLS)),
                      _layer_resident((ATTN_WIDTH + RET_WIDTH, D_MODEL)),
                      _layer_resident((1, D_MODEL)),
                      _layer_resident((1, D_MODEL)),
                      tab_spec,
                      _resident((CHUNK, RET_HEADS * CHUNK)),
                      _resident((CHUNK, RET_WIDTH)),
                      _resident((CHUNK, RET_WIDTH)),
                      _resident((1, RET_WIDTH))],
            out_specs=row_spec,
            scratch_shapes=[pltpu.VMEM((ts, ATTN_WIDTH), BF16),
                            pltpu.VMEM((CHUNK + ts, 2 * PAIR), BF16),
                            pltpu.VMEM((CHUNK + ts, 2 * PAIR), BF16),
                            pltpu.VMEM((ts, RET_WIDTH), F32),
                            pltpu.VMEM((ts, RET_WIDTH), F32),
                            pltpu.VMEM((ts, RET_WIDTH), BF16),
                            pltpu.VMEM((ts, RET_WIDTH), F32),
                            pltpu.VMEM((ts, ATTN_WIDTH + RET_WIDTH), BF16),
                            pltpu.VMEM((ts, D_MODEL), F32),
                            pltpu.VMEM((N_PAIRS, LANES, LANES), F32)]),
        out_shape=jax.ShapeDtypeStruct((batch * seq, D_MODEL), F32),
        compiler_params=pltpu.CompilerParams(dimension_semantics=("arbitrary", "arbitrary"),
                                             vmem_limit_bytes=VMEM_LIMIT_BYTES),
        name="mixer",
    )(layer, sinks, x, win, wout, g, b, rot_tables, decay, wq, wk, gch)


def _pick_tile(n, want):
    t = min(n, want)
    assert n % t == 0 and t % CHUNK == 0, (n, t)
    return t


def kernel(x, w_in, w_out, attn_sinks, ffn1_w_gu, ffn1_w_down, ffn2_w_gu, ffn2_w_down,
           ln1_g, ln1_b, ln2_g, ln2_b, ln3_g, ln3_b):
    batch, seq, d = x.shape
    assert d == D_MODEL and seq % CHUNK == 0
    assert w_in.shape[0] == DEPTH
    t = batch * seq
    tm = _pick_tile(t, 1024)
    ts = _pick_tile(seq, 512)

    rot_tables = _rotation_tables(seq)
    ret_tables = _retention_tables()
    w_in_p = _permute_in_proj(w_in)
    w_out_b = w_out.astype(BF16)
    wgu1, wd1 = ffn1_w_gu.astype(BF16), ffn1_w_down.astype(BF16)
    wgu2, wd2 = ffn2_w_gu.astype(BF16), ffn2_w_down.astype(BF16)

    def vecs(a):
        return a.reshape(DEPTH, 1, D_MODEL)

    g1, b1, g2, b2, g3, b3 = (vecs(a) for a in (ln1_g, ln1_b, ln2_g, ln2_b, ln3_g, ln3_b))
    h = x.reshape(t, D_MODEL)
    for l in range(DEPTH):
        layer = jnp.full((1,), l, jnp.int32)
        h = _ffn(layer, h, wgu1, wd1, g1, b1, tm=tm, tf=256, sub=256)
        h = _mixer(layer, h, attn_sinks, w_in_p, w_out_b, g2, b2, rot_tables, ret_tables,
                   batch=batch, seq=seq, ts=ts)
        h = _ffn(layer, h, wgu2, wd2, g3, b3, tm=tm, tf=256, sub=256)
    return h.reshape(batch, seq, D_MODEL)
```

```python
import functools

import numpy as np
import jax
import jax.numpy as jnp
from jax import lax
from jax.experimental import pallas as pl
from jax.experimental.pallas import tpu as pltpu

F32 = jnp.float32
BF16 = jnp.bfloat16

D_MODEL = 1024
DEPTH = 4
HEAD_DIM = 64
HALF = HEAD_DIM // 2
ATTN_Q_HEADS = 8
ATTN_KV_HEADS = 2
RET_HEADS = 8
ATTN_WIDTH = ATTN_Q_HEADS * HEAD_DIM
KV_WIDTH = ATTN_KV_HEADS * HEAD_DIM
RET_WIDTH = RET_HEADS * HEAD_DIM
WINDOW = 128
CHUNK = 128
ROPE_THETA = 10000.0
RET_THETA = 10000.0
D_FF = 2816
LN_EPS = 1e-5
GN_EPS = 1e-6
ALPHA = (2 * DEPTH) ** 0.25
QK_SCALE = HEAD_DIM ** -0.5

LANES = 128
PAIR = 2 * HEAD_DIM
N_PAIRS = RET_HEADS // 2

QA0, KA0, VA0 = 0, 512, 768
QR0, KR0, VR0, GR0 = 1024, 1536, 2048, 2560
IN_COLS = 3072

VMEM_LIMIT_BYTES = 56 * 1024 * 1024


def _permute_in_proj(w_in):
    lead = w_in.shape[:2]
    s_ka, s_va, s_qr = ATTN_WIDTH, ATTN_WIDTH + KV_WIDTH, ATTN_WIDTH + 2 * KV_WIDTH
    s_kr, s_vr = s_qr + RET_WIDTH, s_qr + 2 * RET_WIDTH

    def half_split(w, pairs):
        w = w.reshape(lead + (pairs, 2, 2, HALF))
        return jnp.swapaxes(w, -3, -2).reshape(lead + (pairs * PAIR,))

    def interleaved(w, pairs):
        w = w.reshape(lead + (pairs, 2, HALF, 2))
        return jnp.transpose(w, (0, 1, 2, 5, 3, 4)).reshape(lead + (pairs * PAIR,))

    qa = half_split(w_in[..., :s_ka], ATTN_Q_HEADS // 2)
    ka = w_in[..., s_ka:s_va].reshape(lead + (ATTN_KV_HEADS, 2, 1, HALF))
    ka = jnp.broadcast_to(ka, lead + (ATTN_KV_HEADS, 2, 2, HALF)).reshape(lead + (2 * PAIR,))
    va = w_in[..., s_va:s_qr].reshape(lead + (ATTN_KV_HEADS, 1, HEAD_DIM))
    va = jnp.broadcast_to(va, lead + (ATTN_KV_HEADS, 2, HEAD_DIM)).reshape(lead + (2 * PAIR,))
    qr = interleaved(w_in[..., s_qr:s_kr], N_PAIRS)
    kr = interleaved(w_in[..., s_kr:s_vr], N_PAIRS)
    out = jnp.concatenate([qa, ka, va, qr, kr, w_in[..., s_vr:]], axis=-1).astype(BF16)
    assert out.shape == lead + (IN_COLS,)
    return out


def _rotation_tables(seq):
    pos = jnp.arange(seq, dtype=F32)
    inv_freq = ROPE_THETA ** (-jnp.arange(0, HEAD_DIM, 2, dtype=F32) / HEAD_DIM)
    ang = pos[:, None] * inv_freq[None, :]
    ret_freq = 1.0 / (RET_THETA ** jnp.linspace(0.0, 1.0, HEAD_DIM // 2, dtype=F32))
    rang = pos[:, None] * ret_freq[None, :]

    def lay(c, s):
        return [c, c, c, c, -s, -s, s, s]

    return jnp.concatenate(lay(jnp.cos(ang), jnp.sin(ang)) + lay(jnp.cos(rang), jnp.sin(rang)), axis=1)


def _retention_tables():
    h = jnp.arange(RET_HEADS, dtype=F32)
    log_gamma = jnp.log1p(-jnp.exp2(-5.0 - h))
    idx = jnp.arange(CHUNK, dtype=F32)
    diff = idx[:, None] - idx[None, :]
    decay = jnp.where(diff[None] >= 0,
                      jnp.exp(jnp.maximum(diff, 0.0)[None] * log_gamma[:, None, None]), 0.0)
    decay = jnp.transpose(decay, (1, 0, 2)).reshape(CHUNK, RET_HEADS * CHUNK)
    w_k = jnp.exp((CHUNK - 1.0 - idx)[None, :] * log_gamma[:, None])
    w_q = jnp.exp((idx + 1.0)[None, :] * log_gamma[:, None])
    g_chunk = jnp.exp(CHUNK * log_gamma)

    def rolled(w):
        w = w.T.reshape(CHUNK, N_PAIRS, 2, 1)
        return jnp.broadcast_to(jnp.concatenate([w, w], axis=2), (CHUNK, N_PAIRS, 4, HALF)).reshape(CHUNK, RET_WIDTH)

    g_nat = jnp.repeat(g_chunk, HEAD_DIM)[None, :]
    return decay, rolled(w_q), rolled(w_k), g_nat


def _layer_norm(z, g, b):
    mu = jnp.mean(z, axis=-1, keepdims=True)
    d = z - mu
    var = jnp.mean(d * d, axis=-1, keepdims=True)
    return d * lax.rsqrt(var + LN_EPS) * g + b


def _dot(a, b):
    return jnp.dot(a, b, preferred_element_type=F32)


def _dot_nt(a, b):
    return lax.dot_general(a, b, (((1,), (1,)), ((), ())), preferred_element_type=F32)


def _dot_tn(a, b):
    return lax.dot_general(a, b, (((0,), (0,)), ((), ())), preferred_element_type=F32)


def _ffn_kernel(layer_ref, x_ref, wgu_ref, wd_ref, g_ref, b_ref, o_ref, h_scr, *, tf, sub):
    for r in range(x_ref.shape[0] // sub):
        rs = slice(r * sub, (r + 1) * sub)
        x = x_ref[rs, :]
        xb = x.astype(BF16)
        for c in range(D_FF // tf):
            a = _dot(xb, wgu_ref[:, c * tf:(c + 1) * tf])
            u = _dot(xb, wgu_ref[:, D_FF + c * tf:D_FF + (c + 1) * tf])
            h_scr[rs, c * tf:(c + 1) * tf] = (jax.nn.silu(a) * u).astype(BF16)
        y = _dot(h_scr[rs, :], wd_ref[...])
        o_ref[rs, :] = _layer_norm(ALPHA * x + 0.5 * y, g_ref[...], b_ref[...])


def _resident(shape):
    return pl.BlockSpec(shape, lambda *_: (0,) * len(shape), pipeline_mode=pl.Buffered(1))


def _layer_resident(shape):
    return pl.BlockSpec((None,) + shape, lambda *a: (a[-1][0],) + (0,) * len(shape),
                        pipeline_mode=pl.Buffered(1))


def _ffn(layer, x, wgu, wd, g, b, *, tm, tf, sub):
    t = x.shape[0]
    return pl.pallas_call(
        functools.partial(_ffn_kernel, tf=tf, sub=sub),
        grid_spec=pltpu.PrefetchScalarGridSpec(
            num_scalar_prefetch=1,
            grid=(t // tm,),
            in_specs=[pl.BlockSpec((tm, D_MODEL), lambda i, l: (i, 0)),
                      _layer_resident((D_MODEL, 2 * D_FF)),
                      _layer_resident((D_FF, D_MODEL)),
                      _layer_resident((1, D_MODEL)),
                      _layer_resident((1, D_MODEL))],
            out_specs=pl.BlockSpec((tm, D_MODEL), lambda i, l: (i, 0)),
            scratch_shapes=[pltpu.VMEM((tm, D_FF), BF16)]),
        out_shape=jax.ShapeDtypeStruct((t, D_MODEL), F32),
        compiler_params=pltpu.CompilerParams(dimension_semantics=("arbitrary",),
                                             vmem_limit_bytes=VMEM_LIMIT_BYTES),
        name="ffn",
    )(layer, x, wgu, wd, g, b)


def _rotate(h, cos, sin):
    outs = []
    for j in range(h.shape[1] // LANES):
        hj = h[:, j * LANES:(j + 1) * LANES]
        outs.append(hj * cos + pltpu.roll(hj, PAIR // 2, 1) * sin)
    return jnp.concatenate(outs, axis=1)


def _mixer_kernel(layer_ref, sinks_ref, x_ref, win_ref, wout_ref, g_ref, b_ref,
                  rot_ref,
                  decay_ref, wq_ref, wk_ref, gch_ref,
                  o_ref,
                  qa_s, ka_s, va_s, qr_s, kr_s, vr_s, gr_s, y_s, z_s, state_s, *, ts):
    s_idx = pl.program_id(1)
    n_chunks = ts // CHUNK
    assert n_chunks >= 4 and n_chunks % 2 == 0

    @pl.when(s_idx == 0)
    def _():
        ka_s[0:CHUNK, :] = jnp.zeros((CHUNK, 2 * PAIR), BF16)
        va_s[0:CHUNK, :] = jnp.zeros((CHUNK, 2 * PAIR), BF16)
        state_s[...] = jnp.zeros(state_s.shape, F32)

    xb = x_ref[...].astype(BF16)
    rcos, rsin = rot_ref[:, 0:LANES], rot_ref[:, LANES:2 * LANES]
    h = _dot(xb, win_ref[:, QA0:QA0 + ATTN_WIDTH])
    qa_s[...] = (_rotate(h, rcos, rsin) * QK_SCALE).astype(BF16)
    h = _dot(xb, win_ref[:, KA0:KA0 + 4 * PAIR])
    ka_s[CHUNK:, :] = _rotate(h[:, :2 * PAIR], rcos, rsin).astype(BF16)
    va_s[CHUNK:, :] = h[:, 2 * PAIR:].astype(BF16)

    def tcos():
        return rot_ref[:, 2 * LANES:3 * LANES]

    def tsin():
        return rot_ref[:, 3 * LANES:4 * LANES]

    def retention_projection(j):
        if j == 0:
            qr_s[...] = _rotate(_dot(xb, win_ref[:, QR0:QR0 + RET_WIDTH]), tcos(), tsin())
        elif j == 1:
            kr_s[...] = _rotate(_dot(xb, win_ref[:, KR0:KR0 + RET_WIDTH]), tcos(), tsin()) * QK_SCALE
        elif j == 2:
            vr_s[...] = _dot(xb, win_ref[:, VR0:VR0 + RET_WIDTH]).astype(BF16)
        else:
            gr_s[...] = _dot(xb, win_ref[:, GR0:GR0 + RET_WIDTH])

    lane = lax.broadcasted_iota(jnp.int32, (1, LANES), 1)
    slot_a = (lane % HEAD_DIM) < HALF
    lo = lane < HEAD_DIM
    row = lax.broadcasted_iota(jnp.int32, (LANES, LANES), 0)
    col = lax.broadcasted_iota(jnp.int32, (LANES, LANES), 1)
    state_mask = ((row % HEAD_DIM) < HALF) == (col < HEAD_DIM)
    gn_avg = jnp.where((row < HEAD_DIM) == (col < HEAD_DIM), 1.0 / HEAD_DIM, 0.0).astype(BF16)
    qi = lax.broadcasted_iota(jnp.int32, (CHUNK, 2 * CHUNK), 0)
    kk = lax.broadcasted_iota(jnp.int32, (CHUNK, 2 * CHUNK), 1)
    band = (kk > qi) & (kk <= qi + WINDOW)

    gn_avg2 = jnp.concatenate([gn_avg, gn_avg], axis=0)

    def half_means(v):
        hi = v.astype(BF16)
        lo_part = (v - hi.astype(F32)).astype(BF16)
        return _dot(jnp.concatenate([hi, lo_part], axis=1), gn_avg2)

    att = [dict() for _ in range(n_chunks)]

    def att_scores(c):
        rows = slice(c * CHUNK, (c + 1) * CHUNK)
        kv_rows = slice(c * CHUNK, (c + 2) * CHUNK)
        s_att, v_att = [], []
        for g in range(ATTN_KV_HEADS):
            kd = ka_s[kv_rows, g * PAIR:(g + 1) * PAIR]
            vd = va_s[kv_rows, g * PAIR:(g + 1) * PAIR]
            zero = jnp.zeros_like(kd)
            k_cat = jnp.concatenate([jnp.where(slot_a, kd, zero), jnp.where(slot_a, zero, kd)], axis=0)
            v_att.append(jnp.concatenate([jnp.where(lo, vd, zero), jnp.where(lo, zero, vd)], axis=0))
            q_st = jnp.concatenate([qa_s[rows, (2 * g) * PAIR:(2 * g + 1) * PAIR],
                                    qa_s[rows, (2 * g + 1) * PAIR:(2 * g + 2) * PAIR]], axis=0)
            s_att.append(_dot_nt(q_st, k_cat))
        att[c].update(s=s_att, v=v_att)

    def att_values(c):
        rows = slice(c * CHUNK, (c + 1) * CHUNK)
        mask = band & ((kk >= CHUNK) | (s_idx > 0)) if c == 0 else band
        s_att, v_att = att[c]["s"], att[c]["v"]
        for g in range(ATTN_KV_HEADS):
            p_rows = []
            for pp in range(2):
                p_cols = []
                for hh in range(2):
                    sink = sinks_ref[layer_ref[0], 2 * (2 * g + pp) + hh]
                    s = s_att[g][pp * CHUNK:(pp + 1) * CHUNK, hh * 2 * CHUNK:(hh + 1) * 2 * CHUNK]
                    s = jnp.where(mask, s, -jnp.inf)
                    m = jnp.maximum(jnp.max(s, axis=-1, keepdims=True), sink)
                    e = jnp.exp(s - m)
                    denom = jnp.sum(e, axis=-1, keepdims=True) + jnp.exp(sink - m)
                    p_cols.append((e / denom).astype(BF16))
                p_rows.append(jnp.concatenate(p_cols, axis=1))
            o = _dot(jnp.concatenate(p_rows, axis=0), v_att[g])
            y_s[rows, (2 * g) * PAIR:(2 * g + 1) * PAIR] = o[:CHUNK].astype(BF16)
            y_s[rows, (2 * g + 1) * PAIR:(2 * g + 2) * PAIR] = o[CHUNK:].astype(BF16)

    att_scores(0)
    retention_projection(0)
    for c in range(1, n_chunks):
        att_scores(c)
        att_values(c - 1)
        if c < 4:
            retention_projection(c)
    att_values(n_chunks - 1)

    ka_s[0:CHUNK, :] = ka_s[ts:ts + CHUNK, :]
    va_s[0:CHUNK, :] = va_s[ts:ts + CHUNK, :]

    staged = [dict() for _ in range(n_chunks)]

    def ret_scores(c):
        rows = slice(c * CHUNK, (c + 1) * CHUNK)
        s_ret, rhs_ret, qw_ret = [], [], []
        for p in range(N_PAIRS):
            cols = slice(p * PAIR, (p + 1) * PAIR)
            q = qr_s[rows, cols]
            k = kr_s[rows, cols]
            v = vr_s[rows, cols]
            kb = k.astype(BF16)
            zero = jnp.zeros_like(kb)
            k_cat = jnp.concatenate([jnp.where(slot_a, kb, zero), jnp.where(slot_a, zero, kb)], axis=0)
            s_ret.append(_dot_nt(q.astype(BF16), k_cat))
            st = state_s[p]
            rhs_ret.append(jnp.concatenate([jnp.where(lo, v, zero), jnp.where(lo, zero, v),
                                            st.astype(BF16)], axis=0))
            qw_ret.append((q * wq_ref[:, cols]).astype(BF16))
            kv = _dot_tn((k * wk_ref[:, cols]).astype(BF16), v)
            state_s[p] = st * gch_ref[:, cols] + jnp.where(state_mask, kv, 0.0)
        z_s[rows, :] = _dot(y_s[rows, 0:ATTN_WIDTH], wout_ref[0:ATTN_WIDTH, :])
        staged[c].update(s=s_ret, rhs=rhs_ret, qw=qw_ret)

    def ret_outputs(c):
        st = staged[c]
        o_ret = []
        for p in range(N_PAIRS):
            sd = (st["s"][p] * decay_ref[:, (2 * p) * CHUNK:(2 * p + 2) * CHUNK]).astype(BF16)
            o_ret.append(_dot(jnp.concatenate([sd, st["qw"][p]], axis=1), st["rhs"][p]))
        st["o"] = jnp.concatenate(o_ret, axis=0)

    def gn_center(c):
        staged[c]["d"] = staged[c]["o"] - half_means(staged[c]["o"])

    def gn_finish(c):
        rows = slice(c * CHUNK, (c + 1) * CHUNK)
        d = staged[c]["d"]
        on = d * lax.rsqrt(half_means(d * d) + GN_EPS)
        for p in range(N_PAIRS):
            cols = slice(p * PAIR, (p + 1) * PAIR)
            y_s[rows, ATTN_WIDTH + p * PAIR:ATTN_WIDTH + (p + 1) * PAIR] = (
                jax.nn.silu(gr_s[rows, cols]) * on[p * CHUNK:(p + 1) * CHUNK]).astype(BF16)

    def finish_rows(c):
        gn_finish(c)
        if c % 2 == 1:
            rs = slice((c - 1) * CHUNK, (c + 1) * CHUNK)
            y = z_s[rs, :] + _dot(y_s[rs, ATTN_WIDTH:], wout_ref[ATTN_WIDTH:, :])
            o_ref[rs, :] = _layer_norm(ALPHA * x_ref[rs, :] + y, g_ref[...], b_ref[...])

    ret_scores(0)
    ret_outputs(0)
    for c in range(1, n_chunks):
        ret_scores(c)
        gn_center(c - 1)
        ret_outputs(c)
        finish_rows(c - 1)
    gn_center(n_chunks - 1)
    finish_rows(n_chunks - 1)


def _mixer(layer, x, sinks, win, wout, g, b, rot_tables, ret_tables, *, batch, seq, ts):
    n_s = seq // ts
    decay, wq, wk, gch = ret_tables
    row_spec = pl.BlockSpec((ts, D_MODEL), lambda bi, si, l: (bi * n_s + si, 0))
    tab_spec = pl.BlockSpec((ts, 4 * LANES), lambda bi, si, l: (si, 0))
    return pl.pallas_call(
        functools.partial(_mixer_kernel, ts=ts),
        grid_spec=pltpu.PrefetchScalarGridSpec(
            num_scalar_prefetch=1,
            grid=(batch, n_s),
            in_specs=[pl.BlockSpec(memory_space=pltpu.SMEM),
                      row_spec,
                      _layer_resident((D_MODEL, IN_COLS)),
                      _layer_resident((ATTN_WIDTH + RET_WIDTH, D_MODEL)),
                      _layer_resident((1, D_MODEL)),
                      _layer_resident((1, D_MODEL)),
                      tab_spec,
                      _resident((CHUNK, RET_HEADS * CHUNK)),
                      _resident((CHUNK, RET_WIDTH)),
                      _resident((CHUNK, RET_WIDTH)),
                      _resident((1, RET_WIDTH))],
            out_specs=row_spec,
            scratch_shapes=[pltpu.VMEM((ts, ATTN_WIDTH), BF16),
                            pltpu.VMEM((CHUNK + ts, 2 * PAIR), BF16),
                            pltpu.VMEM((CHUNK + ts, 2 * PAIR), BF16),
                            pltpu.VMEM((ts, RET_WIDTH), F32),
                            pltpu.VMEM((ts, RET_WIDTH), F32),
                            pltpu.VMEM((ts, RET_WIDTH), BF16),
                            pltpu.VMEM((ts, RET_WIDTH), F32),
                            pltpu.VMEM((ts, ATTN_WIDTH + RET_WIDTH), BF16),
                            pltpu.VMEM((ts, D_MODEL), F32),
                            pltpu.VMEM((N_PAIRS, LANES, LANES), F32)]),
        out_shape=jax.ShapeDtypeStruct((batch * seq, D_MODEL), F32),
        compiler_params=pltpu.CompilerParams(dimension_semantics=("arbitrary", "arbitrary"),
                                             vmem_limit_bytes=VMEM_LIMIT_BYTES),
        name="mixer",
    )(layer, sinks, x, win, wout, g, b, rot_tables, decay, wq, wk, gch)


def _pick_tile(n, want):
    t = min(n, want)
    assert n % t == 0 and t % CHUNK == 0, (n, t)
    return t


def kernel(x, w_in, w_out, attn_sinks, ffn1_w_gu, ffn1_w_down, ffn2_w_gu, ffn2_w_down,
           ln1_g, ln1_b, ln2_g, ln2_b, ln3_g, ln3_b):
    batch, seq, d = x.shape
    assert d == D_MODEL and seq % CHUNK == 0
    assert w_in.shape[0] == DEPTH
    t = batch * seq
    tm = _pick_tile(t, 1024)
    ts = _pick_tile(seq, 512)

    rot_tables = _rotation_tables(seq)
    ret_tables = _retention_tables()
    w_in_p = _permute_in_proj(w_in)
    w_out_b = w_out.astype(BF16)
    wgu1, wd1 = ffn1_w_gu.astype(BF16), ffn1_w_down.astype(BF16)
    wgu2, wd2 = ffn2_w_gu.astype(BF16), ffn2_w_down.astype(BF16)

    def vecs(a):
        return a.reshape(DEPTH, 1, D_MODEL)

    g1, b1, g2, b2, g3, b3 = (vecs(a) for a in (ln1_g, ln1_b, ln2_g, ln2_b, ln3_g, ln3_b))
    h = x.reshape(t, D_MODEL)
    for l in range(DEPTH):
        layer = jnp.full((1,), l, jnp.int32)
        h = _ffn(layer, h, wgu1, wd1, g1, b1, tm=tm, tf=256, sub=256)
        h = _mixer(layer, h, attn_sinks, w_in_p, w_out_b, g2, b2, rot_tables, ret_tables,
                   batch=batch, seq=seq, ts=ts)
        h = _ffn(layer, h, wgu2, wd2, g3, b3, tm=tm, tf=256, sub=256)
    return h.reshape(batch, seq, D_MODEL)
```

```python
import functools

import numpy as np
import jax
import jax.numpy as jnp
from jax import lax
from jax.experimental import pallas as pl
from jax.experimental.pallas import tpu as pltpu

F32 = jnp.float32
BF16 = jnp.bfloat16

D_MODEL = 1024
DEPTH = 4
HEAD_DIM = 64
HALF = HEAD_DIM // 2
ATTN_Q_HEADS = 8
ATTN_KV_HEADS = 2
RET_HEADS = 8
ATTN_WIDTH = ATTN_Q_HEADS * HEAD_DIM
KV_WIDTH = ATTN_KV_HEADS * HEAD_DIM
RET_WIDTH = RET_HEADS * HEAD_DIM
WINDOW = 128
CHUNK = 128
ROPE_THETA = 10000.0
RET_THETA = 10000.0
D_FF = 2816
LN_EPS = 1e-5
GN_EPS = 1e-6
ALPHA = (2 * DEPTH) ** 0.25
QK_SCALE = HEAD_DIM ** -0.5

LANES = 128
PAIR = 2 * HEAD_DIM
N_PAIRS = RET_HEADS // 2

QA0, KA0, VA0 = 0, 512, 768
QR0, KR0, VR0, GR0 = 1024, 1536, 2048, 2560
IN_COLS = 3072

VMEM_LIMIT_BYTES = 56 * 1024 * 1024


def _permute_in_proj(w_in):
    lead = w_in.shape[:2]
    s_ka, s_va, s_qr = ATTN_WIDTH, ATTN_WIDTH + KV_WIDTH, ATTN_WIDTH + 2 * KV_WIDTH
    s_kr, s_vr = s_qr + RET_WIDTH, s_qr + 2 * RET_WIDTH

    def half_split(w, pairs):
        w = w.reshape(lead + (pairs, 2, 2, HALF))
        return jnp.swapaxes(w, -3, -2).reshape(lead + (pairs * PAIR,))

    def interleaved(w, pairs):
        w = w.reshape(lead + (pairs, 2, HALF, 2))
        return jnp.transpose(w, (0, 1, 2, 5, 3, 4)).reshape(lead + (pairs * PAIR,))

    qa = half_split(w_in[..., :s_ka], ATTN_Q_HEADS // 2)
    ka = w_in[..., s_ka:s_va].reshape(lead + (ATTN_KV_HEADS, 2, 1, HALF))
    ka = jnp.broadcast_to(ka, lead + (ATTN_KV_HEADS, 2, 2, HALF)).reshape(lead + (2 * PAIR,))
    va = w_in[..., s_va:s_qr].reshape(lead + (ATTN_KV_HEADS, 1, HEAD_DIM))
    va = jnp.broadcast_to(va, lead + (ATTN_KV_HEADS, 2, HEAD_DIM)).reshape(lead + (2 * PAIR,))
    qr = interleaved(w_in[..., s_qr:s_kr], N_PAIRS)
    kr = interleaved(w_in[..., s_kr:s_vr], N_PAIRS)
    out = jnp.concatenate([qa, ka, va, qr, kr, w_in[..., s_vr:]], axis=-1).astype(BF16)
    assert out.shape == lead + (IN_COLS,)
    return out


def _rotation_tables(seq):
    pos = jnp.arange(seq, dtype=F32)
    inv_freq = ROPE_THETA ** (-jnp.arange(0, HEAD_DIM, 2, dtype=F32) / HEAD_DIM)
    ang = pos[:, None] * inv_freq[None, :]
    ret_freq = 1.0 / (RET_THETA ** jnp.linspace(0.0, 1.0, HEAD_DIM // 2, dtype=F32))
    rang = pos[:, None] * ret_freq[None, :]

    def lay(c, s):
        return [c, c, c, c, -s, -s, s, s]

    return jnp.concatenate(lay(jnp.cos(ang), jnp.sin(ang)) + lay(jnp.cos(rang), jnp.sin(rang)), axis=1)


def _retention_tables():
    h = jnp.arange(RET_HEADS, dtype=F32)
    log_gamma = jnp.log1p(-jnp.exp2(-5.0 - h))
    idx = jnp.arange(CHUNK, dtype=F32)
    diff = idx[:, None] - idx[None, :]
    decay = jnp.where(diff[None] >= 0,
                      jnp.exp(jnp.maximum(diff, 0.0)[None] * log_gamma[:, None, None]), 0.0)
    decay = jnp.transpose(decay, (1, 0, 2)).reshape(CHUNK, RET_HEADS * CHUNK)
    w_k = jnp.exp((CHUNK - 1.0 - idx)[None, :] * log_gamma[:, None])
    w_q = jnp.exp((idx + 1.0)[None, :] * log_gamma[:, None])
    g_chunk = jnp.exp(CHUNK * log_gamma)

    def rolled(w):
        w = w.T.reshape(CHUNK, N_PAIRS, 2, 1)
        return jnp.broadcast_to(jnp.concatenate([w, w], axis=2), (CHUNK, N_PAIRS, 4, HALF)).reshape(CHUNK, RET_WIDTH)

    g_nat = jnp.repeat(g_chunk, HEAD_DIM)[None, :]
    return decay, rolled(w_q), rolled(w_k), g_nat


def _layer_norm(z, g, b):
    mu = jnp.mean(z, axis=-1, keepdims=True)
    d = z - mu
    var = jnp.mean(d * d, axis=-1, keepdims=True)
    return d * lax.rsqrt(var + LN_EPS) * g + b


def _dot(a, b):
    return jnp.dot(a, b, preferred_element_type=F32)


def _dot_nt(a, b):
    return lax.dot_general(a, b, (((1,), (1,)), ((), ())), preferred_element_type=F32)


def _dot_tn(a, b):
    return lax.dot_general(a, b, (((0,), (0,)), ((), ())), preferred_element_type=F32)


def _ffn_kernel(layer_ref, x_ref, wgu_ref, wd_ref, g_ref, b_ref, o_ref, h_scr, *, tf, sub):
    for r in range(x_ref.shape[0] // sub):
        rs = slice(r * sub, (r + 1) * sub)
        x = x_ref[rs, :]
        xb = x.astype(BF16)
        for c in range(D_FF // tf):
            a = _dot(xb, wgu_ref[:, c * tf:(c + 1) * tf])
            u = _dot(xb, wgu_ref[:, D_FF + c * tf:D_FF + (c + 1) * tf])
            h_scr[rs, c * tf:(c + 1) * tf] = (jax.nn.silu(a) * u).astype(BF16)
        y = _dot(h_scr[rs, :], wd_ref[...])
        o_ref[rs, :] = _layer_norm(ALPHA * x + 0.5 * y, g_ref[...], b_ref[...])


def _resident(shape):
    return pl.BlockSpec(shape, lambda *_: (0,) * len(shape), pipeline_mode=pl.Buffered(1))


def _layer_resident(shape):
    return pl.BlockSpec((None,) + shape, lambda *a: (a[-1][0],) + (0,) * len(shape),
                        pipeline_mode=pl.Buffered(1))


def _ffn(layer, x, wgu, wd, g, b, *, tm, tf, sub):
    t = x.shape[0]
    return pl.pallas_call(
        functools.partial(_ffn_kernel, tf=tf, sub=sub),
        grid_spec=pltpu.PrefetchScalarGridSpec(
            num_scalar_prefetch=1,
            grid=(t // tm,),
            in_specs=[pl.BlockSpec((tm, D_MODEL), lambda i, l: (i, 0)),
                      _layer_resident((D_MODEL, 2 * D_FF)),
                      _layer_resident((D_FF, D_MODEL)),
                      _layer_resident((1, D_MODEL)),
                      _layer_resident((1, D_MODEL))],
            out_specs=pl.BlockSpec((tm, D_MODEL), lambda i, l: (i, 0)),
            scratch_shapes=[pltpu.VMEM((tm, D_FF), BF16)]),
        out_shape=jax.ShapeDtypeStruct((t, D_MODEL), F32),
        compiler_params=pltpu.CompilerParams(dimension_semantics=("arbitrary",),
                                             vmem_limit_bytes=VMEM_LIMIT_BYTES),
        name="ffn",
    )(layer, x, wgu, wd, g, b)


def _rotate(h, cos, sin):
    outs = []
    for j in range(h.shape[1] // LANES):
        hj = h[:, j * LANES:(j + 1) * LANES]
        outs.append(hj * cos + pltpu.roll(hj, PAIR // 2, 1) * sin)
    return jnp.concatenate(outs, axis=1)


def _mixer_kernel(layer_ref, sinks_ref, x_ref, win_ref, wout_ref, g_ref, b_ref,
                  rot_ref,
                  decay_ref, wq_ref, wk_ref, gch_ref,
                  o_ref,
                  qa_s, ka_s, va_s, qr_s, kr_s, vr_s, gr_s, y_s, z_s, state_s, *, ts):
    s_idx = pl.program_id(1)
    n_chunks = ts // CHUNK
    assert n_chunks >= 4 and n_chunks % 2 == 0

    @pl.when(s_idx == 0)
    def _():
        ka_s[0:CHUNK, :] = jnp.zeros((CHUNK, 2 * PAIR), BF16)
        va_s[0:CHUNK, :] = jnp.zeros((CHUNK, 2 * PAIR), BF16)
        state_s[...] = jnp.zeros(state_s.shape, F32)

    xb = x_ref[...].astype(BF16)
    rcos, rsin = rot_ref[:, 0:LANES], rot_ref[:, LANES:2 * LANES]
    h = _dot(xb, win_ref[:, QA0:QA0 + ATTN_WIDTH])
    qa_s[...] = (_rotate(h, rcos, rsin) * QK_SCALE).astype(BF16)
    h = _dot(xb, win_ref[:, KA0:KA0 + 4 * PAIR])
    ka_s[CHUNK:, :] = _rotate(h[:, :2 * PAIR], rcos, rsin).astype(BF16)
    va_s[CHUNK:, :] = h[:, 2 * PAIR:].astype(BF16)

    def tcos():
        return rot_ref[:, 2 * LANES:3 * LANES]

    def tsin():
        return rot_ref[:, 3 * LANES:4 * LANES]

    def retention_projection(j):
        if j == 0:
            qr_s[...] = _rotate(_dot(xb, win_ref[:, QR0:QR0 + RET_WIDTH]), tcos(), tsin())
        elif j == 1:
            kr_s[...] = _rotate(_dot(xb, win_ref[:, KR0:KR0 + RET_WIDTH]), tcos(), tsin()) * QK_SCALE
        elif j == 2:
            vr_s[...] = _dot(xb, win_ref[:, VR0:VR0 + RET_WIDTH]).astype(BF16)
        else:
            gr_s[...] = _dot(xb, win_ref[:, GR0:GR0 + RET_WIDTH])

    lane = lax.broadcasted_iota(jnp.int32, (1, LANES), 1)
    slot_a = (lane % HEAD_DIM) < HALF
    lo = lane < HEAD_DIM
    row = lax.broadcasted_iota(jnp.int32, (LANES, LANES), 0)
    col = lax.broadcasted_iota(jnp.int32, (LANES, LANES), 1)
    state_mask = ((row % HEAD_DIM) < HALF) == (col < HEAD_DIM)
    gn_avg = jnp.where((row < HEAD_DIM) == (col < HEAD_DIM), 1.0 / HEAD_DIM, 0.0).astype(BF16)
    qi = lax.broadcasted_iota(jnp.int32, (CHUNK, 2 * CHUNK), 0)
    kk = lax.broadcasted_iota(jnp.int32, (CHUNK, 2 * CHUNK), 1)
    band = (kk > qi) & (kk <= qi + WINDOW)

    gn_avg2 = jnp.concatenate([gn_avg, gn_avg], axis=0)

    def half_means(v):
        hi = v.astype(BF16)
        lo_part = (v - hi.astype(F32)).astype(BF16)
        return _dot(jnp.concatenate([hi, lo_part], axis=1), gn_avg2)

    att = [dict() for _ in range(n_chunks)]

    def att_scores(c):
        rows = slice(c * CHUNK, (c + 1) * CHUNK)
        kv_rows = slice(c * CHUNK, (c + 2) * CHUNK)
        s_att, v_att = [], []
        for g in range(ATTN_KV_HEADS):
            kd = ka_s[kv_rows, g * PAIR:(g + 1) * PAIR]
            vd = va_s[kv_rows, g * PAIR:(g + 1) * PAIR]
            zero = jnp.zeros_like(kd)
            k_cat = jnp.concatenate([jnp.where(slot_a, kd, zero), jnp.where(slot_a, zero, kd)], axis=0)
            v_att.append(jnp.concatenate([jnp.where(lo, vd, zero), jnp.where(lo, zero, vd)], axis=0))
            q_st = jnp.concatenate([qa_s[rows, (2 * g) * PAIR:(2 * g + 1) * PAIR],
                                    qa_s[rows, (2 * g + 1) * PAIR:(2 * g + 2) * PAIR]], axis=0)
            s_att.append(_dot_nt(q_st, k_cat))
        att[c].update(s=s_att, v=v_att)

    def att_values(c):
        rows = slice(c * CHUNK, (c + 1) * CHUNK)
        mask = band & ((kk >= CHUNK) | (s_idx > 0)) if c == 0 else band
        s_att, v_att = att[c]["s"], att[c]["v"]
        for g in range(ATTN_KV_HEADS):
            p_rows = []
            for pp in range(2):
                p_cols = []
                for hh in range(2):
                    sink = sinks_ref[layer_ref[0], 2 * (2 * g + pp) + hh]
                    s = s_att[g][pp * CHUNK:(pp + 1) * CHUNK, hh * 2 * CHUNK:(hh + 1) * 2 * CHUNK]
                    s = jnp.where(mask, s, -jnp.inf)
                    m = jnp.maximum(jnp.max(s, axis=-1, keepdims=True), sink)
                    e = jnp.exp(s - m)
                    denom = jnp.sum(e, axis=-1, keepdims=True) + jnp.exp(sink - m)
                    p_cols.append((e / denom).astype(BF16))
                p_rows.append(jnp.concatenate(p_cols, axis=1))
            o = _dot(jnp.concatenate(p_rows, axis=0), v_att[g])
            y_s[rows, (2 * g) * PAIR:(2 * g + 1) * PAIR] = o[:CHUNK].astype(BF16)
            y_s[rows, (2 * g + 1) * PAIR:(2 * g + 2) * PAIR] = o[CHUNK:].astype(BF16)

    for c in range(n_chunks):
        att_scores(c)
        if c < 4:
            retention_projection(c)
        att_values(c)

    ka_s[0:CHUNK, :] = ka_s[ts:ts + CHUNK, :]
    va_s[0:CHUNK, :] = va_s[ts:ts + CHUNK, :]

    staged = [dict() for _ in range(n_chunks)]

    def ret_scores(c):
        rows = slice(c * CHUNK, (c + 1) * CHUNK)
        s_ret, rhs_ret, qw_ret = [], [], []
        for p in range(N_PAIRS):
            cols = slice(p * PAIR, (p + 1) * PAIR)
            q = qr_s[rows, cols]
            k = kr_s[rows, cols]
            v = vr_s[rows, cols]
            kb = k.astype(BF16)
            zero = jnp.zeros_like(kb)
            k_cat = jnp.concatenate([jnp.where(slot_a, kb, zero), jnp.where(slot_a, zero, kb)], axis=0)
            s_ret.append(_dot_nt(q.astype(BF16), k_cat))
            st = state_s[p]
            rhs_ret.append(jnp.concatenate([jnp.where(lo, v, zero), jnp.where(lo, zero, v),
                                            st.astype(BF16)], axis=0))
            qw_ret.append((q * wq_ref[:, cols]).astype(BF16))
            kv = _dot_tn((k * wk_ref[:, cols]).astype(BF16), v)
            state_s[p] = st * gch_ref[:, cols] + jnp.where(state_mask, kv, 0.0)
        z_s[rows, :] = _dot(y_s[rows, 0:ATTN_WIDTH], wout_ref[0:ATTN_WIDTH, :])
        staged[c].update(s=s_ret, rhs=rhs_ret, qw=qw_ret)

    def ret_outputs(c):
        st = staged[c]
        o_ret = []
        for p in range(N_PAIRS):
            sd = (st["s"][p] * decay_ref[:, (2 * p) * CHUNK:(2 * p + 2) * CHUNK]).astype(BF16)
            o_ret.append(_dot(jnp.concatenate([sd, st["qw"][p]], axis=1), st["rhs"][p]))
        st["o"] = jnp.concatenate(o_ret, axis=0)

    def gn_center(c):
        staged[c]["d"] = staged[c]["o"] - half_means(staged[c]["o"])

    def gn_finish(c):
        rows = slice(c * CHUNK, (c + 1) * CHUNK)
        d = staged[c]["d"]
        on = d * lax.rsqrt(half_means(d * d) + GN_EPS)
        for p in range(N_PAIRS):
            cols = slice(p * PAIR, (p + 1) * PAIR)
            y_s[rows, ATTN_WIDTH + p * PAIR:ATTN_WIDTH + (p + 1) * PAIR] = (
                jax.nn.silu(gr_s[rows, cols]) * on[p * CHUNK:(p + 1) * CHUNK]).astype(BF16)

    def finish_rows(c):
        gn_finish(c)
        if c % 2 == 1:
            rs = slice((c - 1) * CHUNK, (c + 1) * CHUNK)
            y = z_s[rs, :] + _dot(y_s[rs, ATTN_WIDTH:], wout_ref[ATTN_WIDTH:, :])
            o_ref[rs, :] = _layer_norm(ALPHA * x_ref[rs, :] + y, g_ref[...], b_ref[...])

    ret_scores(0)
    ret_outputs(0)
    for c in range(1, n_chunks):
        ret_scores(c)
        gn_center(c - 1)
        ret_outputs(c)
        finish_rows(c - 1)
    gn_center(n_chunks - 1)
    finish_rows(n_chunks - 1)


def _mixer(layer, x, sinks, win, wout, g, b, rot_tables, ret_tables, *, batch, seq, ts):
    n_s = seq // ts
    decay, wq, wk, gch = ret_tables
    row_spec = pl.BlockSpec((ts, D_MODEL), lambda bi, si, l: (bi * n_s + si, 0))
    tab_spec = pl.BlockSpec((ts, 4 * LANES), lambda bi, si, l: (si, 0))
    return pl.pallas_call(
        functools.partial(_mixer_kernel, ts=ts),
        grid_spec=pltpu.PrefetchScalarGridSpec(
            num_scalar_prefetch=1,
            grid=(batch, n_s),
            in_specs=[pl.BlockSpec(memory_space=pltpu.SMEM),
                      row_spec,
                      _layer_resident((D_MODEL, IN_COLS)),
                      _layer_resident((ATTN_WIDTH + RET_WIDTH, D_MODEL)),
                      _layer_resident((1, D_MODEL)),
                      _layer_resident((1, D_MODEL)),
                      tab_spec,
                      _resident((CHUNK, RET_HEADS * CHUNK)),
                      _resident((CHUNK, RET_WIDTH)),
                      _resident((CHUNK, RET_WIDTH)),
                      _resident((1, RET_WIDTH))],
            out_specs=row_spec,
            scratch_shapes=[pltpu.VMEM((ts, ATTN_WIDTH), BF16),
                            pltpu.VMEM((CHUNK + ts, 2 * PAIR), BF16),
                            pltpu.VMEM((CHUNK + ts, 2 * PAIR), BF16),
                            pltpu.VMEM((ts, RET_WIDTH), F32),
                            pltpu.VMEM((ts, RET_WIDTH), F32),
                            pltpu.VMEM((ts, RET_WIDTH), BF16),
                            pltpu.VMEM((ts, RET_WIDTH), F32),
                            pltpu.VMEM((ts, ATTN_WIDTH + RET_WIDTH), BF16),
                            pltpu.VMEM((ts, D_MODEL), F32),
                            pltpu.VMEM((N_PAIRS, LANES, LANES), F32)]),
        out_shape=jax.ShapeDtypeStruct((batch * seq, D_MODEL), F32),
        compiler_params=pltpu.CompilerParams(dimension_semantics=("arbitrary", "arbitrary"),
                                             vmem_limit_bytes=VMEM_LIMIT_BYTES),
        name="mixer",
    )(layer, sinks, x, win, wout, g, b, rot_tables, decay, wq, wk, gch)


def _pick_tile(n, want):
    t = min(n, want)
    assert n % t == 0 and t % CHUNK == 0, (n, t)
    return t


def kernel(x, w_in, w_out, attn_sinks, ffn1_w_gu, ffn1_w_down, ffn2_w_gu, ffn2_w_down,
           ln1_g, ln1_b, ln2_g, ln2_b, ln3_g, ln3_b):
    batch, seq, d = x.shape
    assert d == D_MODEL and seq % CHUNK == 0
    assert w_in.shape[0] == DEPTH
    t = batch * seq
    tm = _pick_tile(t, 1024)
    ts = _pick_tile(seq, 512)

    rot_tables = _rotation_tables(seq)
    ret_tables = _retention_tables()
    w_in_p = _permute_in_proj(w_in)
    w_out_b = w_out.astype(BF16)
    wgu1, wd1 = ffn1_w_gu.astype(BF16), ffn1_w_down.astype(BF16)
    wgu2, wd2 = ffn2_w_gu.astype(BF16), ffn2_w_down.astype(BF16)

    def vecs(a):
        return a.reshape(DEPTH, 1, D_MODEL)

    g1, b1, g2, b2, g3, b3 = (vecs(a) for a in (ln1_g, ln1_b, ln2_g, ln2_b, ln3_g, ln3_b))
    h = x.reshape(t, D_MODEL)
    for l in range(DEPTH):
        layer = jnp.full((1,), l, jnp.int32)
        h = _ffn(layer, h, wgu1, wd1, g1, b1, tm=tm, tf=256, sub=256)
        h = _mixer(layer, h, attn_sinks, w_in_p, w_out_b, g2, b2, rot_tables, ret_tables,
                   batch=batch, seq=seq, ts=ts)
        h = _ffn(layer, h, wgu2, wd2, g3, b3, tm=tm, tf=256, sub=256)
    return h.reshape(batch, seq, D_MODEL)
```

```python
import functools

import numpy as np
import jax
import jax.numpy as jnp
from jax import lax
from jax.experimental import pallas as pl
from jax.experimental.pallas import tpu as pltpu

F32 = jnp.float32
BF16 = jnp.bfloat16

D_MODEL = 1024
DEPTH = 4
HEAD_DIM = 64
HALF = HEAD_DIM // 2
ATTN_Q_HEADS = 8
ATTN_KV_HEADS = 2
RET_HEADS = 8
ATTN_WIDTH = ATTN_Q_HEADS * HEAD_DIM
KV_WIDTH = ATTN_KV_HEADS * HEAD_DIM
RET_WIDTH = RET_HEADS * HEAD_DIM
WINDOW = 128
CHUNK = 128
ROPE_THETA = 10000.0
RET_THETA = 10000.0
D_FF = 2816
LN_EPS = 1e-5
GN_EPS = 1e-6
ALPHA = (2 * DEPTH) ** 0.25
QK_SCALE = HEAD_DIM ** -0.5

LANES = 128
PAIR = 2 * HEAD_DIM
N_PAIRS = RET_HEADS // 2

QA0, KA0, VA0 = 0, 512, 768
QR0, KR0, VR0, GR0 = 1024, 1536, 2048, 2560
IN_COLS = 3072

VMEM_LIMIT_BYTES = 56 * 1024 * 1024


def _permute_in_proj(w_in):
    lead = w_in.shape[:2]
    s_ka, s_va, s_qr = ATTN_WIDTH, ATTN_WIDTH + KV_WIDTH, ATTN_WIDTH + 2 * KV_WIDTH
    s_kr, s_vr = s_qr + RET_WIDTH, s_qr + 2 * RET_WIDTH

    def half_split(w, pairs):
        w = w.reshape(lead + (pairs, 2, 2, HALF))
        return jnp.swapaxes(w, -3, -2).reshape(lead + (pairs * PAIR,))

    def interleaved(w, pairs):
        w = w.reshape(lead + (pairs, 2, HALF, 2))
        return jnp.transpose(w, (0, 1, 2, 5, 3, 4)).reshape(lead + (pairs * PAIR,))

    qa = half_split(w_in[..., :s_ka], ATTN_Q_HEADS // 2)
    ka = w_in[..., s_ka:s_va].reshape(lead + (ATTN_KV_HEADS, 2, 1, HALF))
    ka = jnp.broadcast_to(ka, lead + (ATTN_KV_HEADS, 2, 2, HALF)).reshape(lead + (2 * PAIR,))
    va = w_in[..., s_va:s_qr].reshape(lead + (ATTN_KV_HEADS, 1, HEAD_DIM))
    va = jnp.broadcast_to(va, lead + (ATTN_KV_HEADS, 2, HEAD_DIM)).reshape(lead + (2 * PAIR,))
    qr = interleaved(w_in[..., s_qr:s_kr], N_PAIRS)
    kr = interleaved(w_in[..., s_kr:s_vr], N_PAIRS)
    out = jnp.concatenate([qa, ka, va, qr, kr, w_in[..., s_vr:]], axis=-1).astype(BF16)
    assert out.shape == lead + (IN_COLS,)
    return out


def _rotation_tables(seq):
    pos = jnp.arange(seq, dtype=F32)
    inv_freq = ROPE_THETA ** (-jnp.arange(0, HEAD_DIM, 2, dtype=F32) / HEAD_DIM)
    ang = pos[:, None] * inv_freq[None, :]
    ret_freq = 1.0 / (RET_THETA ** jnp.linspace(0.0, 1.0, HEAD_DIM // 2, dtype=F32))
    rang = pos[:, None] * ret_freq[None, :]

    def lay(c, s):
        return [c, c, c, c, -s, -s, s, s]

    return jnp.concatenate(lay(jnp.cos(ang), jnp.sin(ang)) + lay(jnp.cos(rang), jnp.sin(rang)), axis=1)


def _retention_tables():
    h = jnp.arange(RET_HEADS, dtype=F32)
    log_gamma = jnp.log1p(-jnp.exp2(-5.0 - h))
    idx = jnp.arange(CHUNK, dtype=F32)
    diff = idx[:, None] - idx[None, :]
    decay = jnp.where(diff[None] >= 0,
                      jnp.exp(jnp.maximum(diff, 0.0)[None] * log_gamma[:, None, None]), 0.0)
    decay = jnp.transpose(decay, (1, 0, 2)).reshape(CHUNK, RET_HEADS * CHUNK)
    w_k = jnp.exp((CHUNK - 1.0 - idx)[None, :] * log_gamma[:, None])
    w_q = jnp.exp((idx + 1.0)[None, :] * log_gamma[:, None])
    g_chunk = jnp.exp(CHUNK * log_gamma)

    def rolled(w):
        w = w.T.reshape(CHUNK, N_PAIRS, 2, 1)
        return jnp.broadcast_to(jnp.concatenate([w, w], axis=2), (CHUNK, N_PAIRS, 4, HALF)).reshape(CHUNK, RET_WIDTH)

    g_nat = jnp.repeat(g_chunk, HEAD_DIM)[None, :]
    return decay, rolled(w_q), rolled(w_k), g_nat


def _layer_norm(z, g, b):
    mu = jnp.mean(z, axis=-1, keepdims=True)
    d = z - mu
    var = jnp.mean(d * d, axis=-1, keepdims=True)
    return d * lax.rsqrt(var + LN_EPS) * g + b


def _dot(a, b):
    return jnp.dot(a, b, preferred_element_type=F32)


def _dot_nt(a, b):
    return lax.dot_general(a, b, (((1,), (1,)), ((), ())), preferred_element_type=F32)


def _dot_tn(a, b):
    return lax.dot_general(a, b, (((0,), (0,)), ((), ())), preferred_element_type=F32)


def _ffn_kernel(layer_ref, x_ref, wgu_ref, wd_ref, g_ref, b_ref, o_ref, h_scr, *, tf, sub):
    n_sub = x_ref.shape[0] // sub

    def rows(r):
        return slice(r * sub, (r + 1) * sub)

    def gate_up(r, xb, c):
        a = _dot(xb, wgu_ref[:, c * tf:(c + 1) * tf])
        u = _dot(xb, wgu_ref[:, D_FF + c * tf:D_FF + (c + 1) * tf])
        h_scr[rows(r), c * tf:(c + 1) * tf] = (jax.nn.silu(a) * u).astype(BF16)

    def down(r):
        y = _dot(h_scr[rows(r), :], wd_ref[...])
        o_ref[rows(r), :] = _layer_norm(ALPHA * x_ref[rows(r), :] + 0.5 * y, g_ref[...], b_ref[...])

    for r in range(n_sub):
        xb = x_ref[rows(r), :].astype(BF16)
        gate_up(r, xb, 0)
        if r > 0:
            down(r - 1)
        for c in range(1, D_FF // tf):
            gate_up(r, xb, c)
    down(n_sub - 1)


def _resident(shape):
    return pl.BlockSpec(shape, lambda *_: (0,) * len(shape), pipeline_mode=pl.Buffered(1))


def _layer_resident(shape):
    return pl.BlockSpec((None,) + shape, lambda *a: (a[-1][0],) + (0,) * len(shape),
                        pipeline_mode=pl.Buffered(1))


def _ffn(layer, x, wgu, wd, g, b, *, tm, tf, sub):
    t = x.shape[0]
    return pl.pallas_call(
        functools.partial(_ffn_kernel, tf=tf, sub=sub),
        grid_spec=pltpu.PrefetchScalarGridSpec(
            num_scalar_prefetch=1,
            grid=(t // tm,),
            in_specs=[pl.BlockSpec((tm, D_MODEL), lambda i, l: (i, 0)),
                      _layer_resident((D_MODEL, 2 * D_FF)),
                      _layer_resident((D_FF, D_MODEL)),
                      _layer_resident((1, D_MODEL)),
                      _layer_resident((1, D_MODEL))],
            out_specs=pl.BlockSpec((tm, D_MODEL), lambda i, l: (i, 0)),
            scratch_shapes=[pltpu.VMEM((tm, D_FF), BF16)]),
        out_shape=jax.ShapeDtypeStruct((t, D_MODEL), F32),
        compiler_params=pltpu.CompilerParams(dimension_semantics=("arbitrary",),
                                             vmem_limit_bytes=VMEM_LIMIT_BYTES),
        name="ffn",
    )(layer, x, wgu, wd, g, b)


def _rotate(h, cos, sin):
    outs = []
    for j in range(h.shape[1] // LANES):
        hj = h[:, j * LANES:(j + 1) * LANES]
        outs.append(hj * cos + pltpu.roll(hj, PAIR // 2, 1) * sin)
    return jnp.concatenate(outs, axis=1)


def _mixer_kernel(layer_ref, sinks_ref, x_ref, win_ref, wout_ref, g_ref, b_ref,
                  rot_ref,
                  decay_ref, wq_ref, wk_ref, gch_ref,
                  o_ref,
                  qa_s, ka_s, va_s, qr_s, kr_s, vr_s, gr_s, y_s, z_s, state_s, *, ts):
    s_idx = pl.program_id(1)
    n_chunks = ts // CHUNK
    assert n_chunks >= 4

    @pl.when(s_idx == 0)
    def _():
        ka_s[0:CHUNK, :] = jnp.zeros((CHUNK, 2 * PAIR), BF16)
        va_s[0:CHUNK, :] = jnp.zeros((CHUNK, 2 * PAIR), BF16)
        state_s[...] = jnp.zeros(state_s.shape, F32)

    xb = x_ref[...].astype(BF16)
    rcos, rsin = rot_ref[:, 0:LANES], rot_ref[:, LANES:2 * LANES]
    h = _dot(xb, win_ref[:, QA0:QA0 + ATTN_WIDTH])
    qa_s[...] = (_rotate(h, rcos, rsin) * QK_SCALE).astype(BF16)
    h = _dot(xb, win_ref[:, KA0:KA0 + 4 * PAIR])
    ka_s[CHUNK:, :] = _rotate(h[:, :2 * PAIR], rcos, rsin).astype(BF16)
    va_s[CHUNK:, :] = h[:, 2 * PAIR:].astype(BF16)

    def tcos():
        return rot_ref[:, 2 * LANES:3 * LANES]

    def tsin():
        return rot_ref[:, 3 * LANES:4 * LANES]

    def retention_projection(j):
        if j == 0:
            qr_s[...] = _rotate(_dot(xb, win_ref[:, QR0:QR0 + RET_WIDTH]), tcos(), tsin())
        elif j == 1:
            kr_s[...] = _rotate(_dot(xb, win_ref[:, KR0:KR0 + RET_WIDTH]), tcos(), tsin()) * QK_SCALE
        elif j == 2:
            vr_s[...] = _dot(xb, win_ref[:, VR0:VR0 + RET_WIDTH]).astype(BF16)
        else:
            gr_s[...] = _dot(xb, win_ref[:, GR0:GR0 + RET_WIDTH])

    lane = lax.broadcasted_iota(jnp.int32, (1, LANES), 1)
    slot_a = (lane % HEAD_DIM) < HALF
    lo = lane < HEAD_DIM
    row = lax.broadcasted_iota(jnp.int32, (LANES, LANES), 0)
    col = lax.broadcasted_iota(jnp.int32, (LANES, LANES), 1)
    state_mask = ((row % HEAD_DIM) < HALF) == (col < HEAD_DIM)
    gn_avg = jnp.where((row < HEAD_DIM) == (col < HEAD_DIM), 1.0 / HEAD_DIM, 0.0).astype(BF16)
    qi = lax.broadcasted_iota(jnp.int32, (CHUNK, 2 * CHUNK), 0)
    kk = lax.broadcasted_iota(jnp.int32, (CHUNK, 2 * CHUNK), 1)
    band = (kk > qi) & (kk <= qi + WINDOW)

    gn_avg2 = jnp.concatenate([gn_avg, gn_avg], axis=0)

    def half_means(v):
        hi = v.astype(BF16)
        lo_part = (v - hi.astype(F32)).astype(BF16)
        return _dot(jnp.concatenate([hi, lo_part], axis=1), gn_avg2)

    for c in range(n_chunks):
        rows = slice(c * CHUNK, (c + 1) * CHUNK)
        kv_rows = slice(c * CHUNK, (c + 2) * CHUNK)
        mask = band & ((kk >= CHUNK) | (s_idx > 0)) if c == 0 else band

        s_att, v_att = [], []
        for g in range(ATTN_KV_HEADS):
            kd = ka_s[kv_rows, g * PAIR:(g + 1) * PAIR]
            vd = va_s[kv_rows, g * PAIR:(g + 1) * PAIR]
            zero = jnp.zeros_like(kd)
            k_cat = jnp.concatenate([jnp.where(slot_a, kd, zero), jnp.where(slot_a, zero, kd)], axis=0)
            v_att.append(jnp.concatenate([jnp.where(lo, vd, zero), jnp.where(lo, zero, vd)], axis=0))
            q_st = jnp.concatenate([qa_s[rows, (2 * g) * PAIR:(2 * g + 1) * PAIR],
                                    qa_s[rows, (2 * g + 1) * PAIR:(2 * g + 2) * PAIR]], axis=0)
            s_att.append(_dot_nt(q_st, k_cat))

        if c < 4:
            retention_projection(c)

        for g in range(ATTN_KV_HEADS):
            p_rows = []
            for pp in range(2):
                p_cols = []
                for hh in range(2):
                    sink = sinks_ref[layer_ref[0], 2 * (2 * g + pp) + hh]
                    s = s_att[g][pp * CHUNK:(pp + 1) * CHUNK, hh * 2 * CHUNK:(hh + 1) * 2 * CHUNK]
                    s = jnp.where(mask, s, -jnp.inf)
                    m = jnp.maximum(jnp.max(s, axis=-1, keepdims=True), sink)
                    e = jnp.exp(s - m)
                    denom = jnp.sum(e, axis=-1, keepdims=True) + jnp.exp(sink - m)
                    p_cols.append((e / denom).astype(BF16))
                p_rows.append(jnp.concatenate(p_cols, axis=1))
            o = _dot(jnp.concatenate(p_rows, axis=0), v_att[g])
            y_s[rows, (2 * g) * PAIR:(2 * g + 1) * PAIR] = o[:CHUNK].astype(BF16)
            y_s[rows, (2 * g + 1) * PAIR:(2 * g + 2) * PAIR] = o[CHUNK:].astype(BF16)

    ka_s[0:CHUNK, :] = ka_s[ts:ts + CHUNK, :]
    va_s[0:CHUNK, :] = va_s[ts:ts + CHUNK, :]

    staged = [dict() for _ in range(n_chunks)]

    def ret_scores(c):
        rows = slice(c * CHUNK, (c + 1) * CHUNK)
        s_ret, rhs_ret, qw_ret = [], [], []
        for p in range(N_PAIRS):
            cols = slice(p * PAIR, (p + 1) * PAIR)
            q = qr_s[rows, cols]
            k = kr_s[rows, cols]
            v = vr_s[rows, cols]
            kb = k.astype(BF16)
            zero = jnp.zeros_like(kb)
            k_cat = jnp.concatenate([jnp.where(slot_a, kb, zero), jnp.where(slot_a, zero, kb)], axis=0)
            s_ret.append(_dot_nt(q.astype(BF16), k_cat))
            st = state_s[p]
            rhs_ret.append(jnp.concatenate([jnp.where(lo, v, zero), jnp.where(lo, zero, v),
                                            st.astype(BF16)], axis=0))
            qw_ret.append((q * wq_ref[:, cols]).astype(BF16))
            kv = _dot_tn((k * wk_ref[:, cols]).astype(BF16), v)
            state_s[p] = st * gch_ref[:, cols] + jnp.where(state_mask, kv, 0.0)
        z_s[rows, :] = _dot(y_s[rows, 0:ATTN_WIDTH], wout_ref[0:ATTN_WIDTH, :])
        staged[c].update(s=s_ret, rhs=rhs_ret, qw=qw_ret)

    def ret_outputs(c):
        st = staged[c]
        o_ret = []
        for p in range(N_PAIRS):
            sd = (st["s"][p] * decay_ref[:, (2 * p) * CHUNK:(2 * p + 2) * CHUNK]).astype(BF16)
            o_ret.append(_dot(jnp.concatenate([sd, st["qw"][p]], axis=1), st["rhs"][p]))
        st["o"] = jnp.concatenate(o_ret, axis=0)

    def gn_center(c):
        staged[c]["d"] = staged[c]["o"] - half_means(staged[c]["o"])

    def gn_finish(c):
        rows = slice(c * CHUNK, (c + 1) * CHUNK)
        d = staged[c]["d"]
        on = d * lax.rsqrt(half_means(d * d) + GN_EPS)
        for p in range(N_PAIRS):
            cols = slice(p * PAIR, (p + 1) * PAIR)
            y_s[rows, ATTN_WIDTH + p * PAIR:ATTN_WIDTH + (p + 1) * PAIR] = (
                jax.nn.silu(gr_s[rows, cols]) * on[p * CHUNK:(p + 1) * CHUNK]).astype(BF16)

    ret_scores(0)
    ret_outputs(0)
    for c in range(1, n_chunks):
        ret_scores(c)
        gn_center(c - 1)
        ret_outputs(c)
        gn_finish(c - 1)
    gn_center(n_chunks - 1)
    gn_finish(n_chunks - 1)

    for r in range(ts // 256):
        rs = slice(r * 256, (r + 1) * 256)
        y = z_s[rs, :] + _dot(y_s[rs, ATTN_WIDTH:], wout_ref[ATTN_WIDTH:, :])
        o_ref[rs, :] = _layer_norm(ALPHA * x_ref[rs, :] + y, g_ref[...], b_ref[...])


def _mixer(layer, x, sinks, win, wout, g, b, rot_tables, ret_tables, *, batch, seq, ts):
    n_s = seq // ts
    decay, wq, wk, gch = ret_tables
    row_spec = pl.BlockSpec((ts, D_MODEL), lambda bi, si, l: (bi * n_s + si, 0))
    tab_spec = pl.BlockSpec((ts, 4 * LANES), lambda bi, si, l: (si, 0))
    return pl.pallas_call(
        functools.partial(_mixer_kernel, ts=ts),
        grid_spec=pltpu.PrefetchScalarGridSpec(
            num_scalar_prefetch=1,
            grid=(batch, n_s),
            in_specs=[pl.BlockSpec(memory_space=pltpu.SMEM),
                      row_spec,
                      _layer_resident((D_MODEL, IN_COLS)),
                      _layer_resident((ATTN_WIDTH + RET_WIDTH, D_MODEL)),
                      _layer_resident((1, D_MODEL)),
                      _layer_resident((1, D_MODEL)),
                      tab_spec,
                      _resident((CHUNK, RET_HEADS * CHUNK)),
                      _resident((CHUNK, RET_WIDTH)),
                      _resident((CHUNK, RET_WIDTH)),
                      _resident((1, RET_WIDTH))],
            out_specs=row_spec,
            scratch_shapes=[pltpu.VMEM((ts, ATTN_WIDTH), BF16),
                            pltpu.VMEM((CHUNK + ts, 2 * PAIR), BF16),
                            pltpu.VMEM((CHUNK + ts, 2 * PAIR), BF16),
                            pltpu.VMEM((ts, RET_WIDTH), F32),
                            pltpu.VMEM((ts, RET_WIDTH), F32),
                            pltpu.VMEM((ts, RET_WIDTH), BF16),
                            pltpu.VMEM((ts, RET_WIDTH), F32),
                            pltpu.VMEM((ts, ATTN_WIDTH + RET_WIDTH), BF16),
                            pltpu.VMEM((ts, D_MODEL), F32),
                            pltpu.VMEM((N_PAIRS, LANES, LANES), F32)]),
        out_shape=jax.ShapeDtypeStruct((batch * seq, D_MODEL), F32),
        compiler_params=pltpu.CompilerParams(dimension_semantics=("arbitrary", "arbitrary"),
                                             vmem_limit_bytes=VMEM_LIMIT_BYTES),
        name="mixer",
    )(layer, sinks, x, win, wout, g, b, rot_tables, decay, wq, wk, gch)


def _pick_tile(n, want):
    t = min(n, want)
    assert n % t == 0 and t % CHUNK == 0, (n, t)
    return t


def kernel(x, w_in, w_out, attn_sinks, ffn1_w_gu, ffn1_w_down, ffn2_w_gu, ffn2_w_down,
           ln1_g, ln1_b, ln2_g, ln2_b, ln3_g, ln3_b):
    batch, seq, d = x.shape
    assert d == D_MODEL and seq % CHUNK == 0
    assert w_in.shape[0] == DEPTH
    t = batch * seq
    tm = _pick_tile(t, 1024)
    ts = _pick_tile(seq, 512)

    rot_tables = _rotation_tables(seq)
    ret_tables = _retention_tables()
    w_in_p = _permute_in_proj(w_in)
    w_out_b = w_out.astype(BF16)
    wgu1, wd1 = ffn1_w_gu.astype(BF16), ffn1_w_down.astype(BF16)
    wgu2, wd2 = ffn2_w_gu.astype(BF16), ffn2_w_down.astype(BF16)

    def vecs(a):
        return a.reshape(DEPTH, 1, D_MODEL)

    g1, b1, g2, b2, g3, b3 = (vecs(a) for a in (ln1_g, ln1_b, ln2_g, ln2_b, ln3_g, ln3_b))
    h = x.reshape(t, D_MODEL)
    for l in range(DEPTH):
        layer = jnp.full((1,), l, jnp.int32)
        h = _ffn(layer, h, wgu1, wd1, g1, b1, tm=tm, tf=256, sub=256)
        h = _mixer(layer, h, attn_sinks, w_in_p, w_out_b, g2, b2, rot_tables, ret_tables,
                   batch=batch, seq=seq, ts=ts)
        h = _ffn(layer, h, wgu2, wd2, g3, b3, tm=tm, tf=256, sub=256)
    return h.reshape(batch, seq, D_MODEL)
```
